```python
import jax, jax.numpy as jnp
from jax import lax
import numpy as np

D_MODEL = 4096
BATCH = 1
SEQ = 8192
DEPTH = 1
DEC_BATCH = 1
DEC_SEQ = 16384
PAST_LEN = 128

GRID_W = 64
EPS = 1e-6
HG_HEADS = 16
HG_DK = 128
HG_DV = 128
HG_WIDTH = HG_HEADS * HG_DK
HG_CHUNK = 64
ATT_Q_HEADS = 16
ATT_KV_HEADS = 4
HEAD_DIM = 128
ATT_WIDTH = ATT_Q_HEADS * HEAD_DIM
KV_WIDTH = ATT_KV_HEADS * HEAD_DIM
Q_BLOCK = 128
ROPE_THETA = 10000.0
ROPE_AXIS_DIM = HEAD_DIM // 2
D_FF = -(-8 * D_MODEL // (3 * 256)) * 256
IN_SIZES = (HG_WIDTH, HG_WIDTH, HG_WIDTH, HG_HEADS * HG_DV, HG_HEADS * HG_DV,
            ATT_WIDTH, KV_WIDTH, KV_WIDTH, D_MODEL, D_MODEL)
IN_WIDTH = sum(IN_SIZES)

kernel_name = 'hybrid_hgrn2_gqa_axial_encoder'


def rmsnorm(x, w):
    xf = x.astype(jnp.float32)
    xf = xf * lax.rsqrt(jnp.mean(xf * xf, axis=-1, keepdims=True) + EPS)
    return xf.astype(x.dtype) * w


def hgrn2_chunk_scan(q, k, v, log_f):
    B, T, H, DK = q.shape
    DV = v.shape[-1]
    n = T // HG_CHUNK

    def to_chunks(a):
        return a.reshape(B, n, HG_CHUNK, H, a.shape[-1]).transpose(1, 0, 3, 2, 4)

    xs = (to_chunks(q), to_chunks(k), to_chunks(v), to_chunks(log_f))
    causal = jnp.tril(jnp.ones((HG_CHUNK, HG_CHUNK), dtype=bool))[:, :, None]

    def step(S, inp):
        qi, ki, vi, gi = inp
        b = jnp.cumsum(gi, axis=-2)
        b_last = b[..., -1:, :]
        o_inter = jnp.einsum('bhtk,bhkv->bhtv', qi * jnp.exp(b), S)
        diff = b[..., :, None, :] - b[..., None, :, :]
        decay = jnp.exp(jnp.where(causal, diff, -jnp.inf))
        scores = jnp.einsum('bhtk,bhsk,bhtsk->bhts', qi, ki, decay)
        o_intra = jnp.einsum('bhts,bhsv->bhtv', scores, vi)
        S_new = jnp.exp(b_last[..., 0, :])[..., None] * S + jnp.einsum(
            'bhsk,bhsv->bhkv', ki * jnp.exp(b_last - b), vi)
        return S_new, o_inter + o_intra

    S0 = jnp.zeros((B, H, DK, DV), jnp.float32)
    _, o = lax.scan(step, S0, xs)
    return o.transpose(1, 0, 3, 2, 4).reshape(B, T, H, DV)


def forget_gate(z, lb):
    f = lb + (1.0 - lb) * jax.nn.sigmoid(z)
    return jnp.log(f), (1.0 - lb) * jax.nn.sigmoid(-z)


def hgrn2_bidirectional(q, z_fwd, z_bwd, i, lb_f, lb_b):
    q, z_fwd, z_bwd, i = (a.astype(jnp.float32) for a in (q, z_fwd, z_bwd, i))
    logf_f, k_f = forget_gate(z_fwd, lb_f)
    logf_b, k_b = forget_gate(z_bwd, lb_b)
    o_f = hgrn2_chunk_scan(q, k_f, i, logf_f)
    flip = lambda a: jnp.flip(a, axis=1)
    o_b = flip(hgrn2_chunk_scan(flip(q), flip(k_b), flip(i), flip(logf_b)))
    return o_f + o_b


def axial_rope_tables(T):
    rows = T // GRID_W
    row = jnp.repeat(jnp.arange(rows, dtype=jnp.float32), GRID_W)
    col = jnp.tile(jnp.arange(GRID_W, dtype=jnp.float32), rows)
    inv = ROPE_THETA ** (-jnp.arange(0, ROPE_AXIS_DIM, 2, dtype=jnp.float32) / ROPE_AXIS_DIM)
    ang = jnp.concatenate([row[:, None] * inv, col[:, None] * inv], axis=-1)
    return jnp.cos(ang), jnp.sin(ang)


def apply_axial_rope(x, cos, sin):
    xf = x.astype(jnp.float32).reshape(*x.shape[:-1], HEAD_DIM // 2, 2)
    x0, x1 = xf[..., 0], xf[..., 1]
    c, s = cos[None, :, None, :], sin[None, :, None, :]
    out = jnp.stack([x0 * c - x1 * s, x0 * s + x1 * c], axis=-1).reshape(x.shape)
    return out.astype(x.dtype)


def gqa_block_attention(q, k, v):
    B, T, Hq, D = q.shape
    G = Hq // ATT_KV_HEADS
    nb = T // Q_BLOCK
    scale = 1.0 / np.sqrt(HEAD_DIM).astype(np.float32)
    qb = q.reshape(B, nb, Q_BLOCK, ATT_KV_HEADS, G, D).transpose(1, 0, 2, 3, 4, 5)

    def block(qi):
        s = jnp.einsum('bqhgd,bkhd->bhgqk', qi, k).astype(jnp.float32) * scale
        p = jax.nn.softmax(s, axis=-1)
        return jnp.einsum('bhgqk,bkhd->bqhgd', p.astype(v.dtype), v)

    o = lax.map(block, qb)
    return o.transpose(1, 0, 2, 3, 4, 5).reshape(B, T, Hq * D)


def encoder_layer(x, norm_mix_w, w_in, lb_f, lb_b, hg_norm_w, q_norm_w, k_norm_w,
                  w_branch_a, w_branch_b, w_out, norm_ffn_w, w_gate_up, w_down):
    B, T, _ = x.shape
    h = rmsnorm(x, norm_mix_w)
    proj = h @ w_in
    splits = np.cumsum(IN_SIZES)[:-1].tolist()
    hq, hf_f, hf_b, hi, hog, aq, ak, av, ga, gb = jnp.split(proj, splits, axis=-1)

    heads = lambda a: a.reshape(B, T, HG_HEADS, -1)
    o_hg = hgrn2_bidirectional(heads(hq), heads(hf_f), heads(hf_b), heads(hi),
                               lb_f.reshape(HG_HEADS, HG_DK), lb_b.reshape(HG_HEADS, HG_DK))
    o_hg = rmsnorm(o_hg.astype(x.dtype), hg_norm_w.reshape(HG_HEADS, HG_DV))
    o_hg = (o_hg * jax.nn.silu(heads(hog))).reshape(B, T, HG_HEADS * HG_DV)
    y_a = o_hg @ w_branch_a

    cos, sin = axial_rope_tables(T)
    q = apply_axial_rope(rmsnorm(aq.reshape(B, T, ATT_Q_HEADS, HEAD_DIM), q_norm_w), cos, sin)
    k = apply_axial_rope(rmsnorm(ak.reshape(B, T, ATT_KV_HEADS, HEAD_DIM), k_norm_w), cos, sin)
    v = av.reshape(B, T, ATT_KV_HEADS, HEAD_DIM)
    y_b = gqa_block_attention(q, k, v) @ w_branch_b

    merged = jax.nn.sigmoid(ga) * y_a + jax.nn.sigmoid(gb) * y_b
    x = x + merged @ w_out

    gu = rmsnorm(x, norm_ffn_w) @ w_gate_up
    g, u = jnp.split(gu, 2, axis=-1)
    return x + (jax.nn.silu(g) * u) @ w_down


def trunk(x, norm_mix_w, w_in, lb_fwd, lb_bwd, hg_norm_w, q_norm_w, k_norm_w,
          w_branch_a, w_branch_b, w_out, norm_ffn_w, w_gate_up, w_down, norm_final_w):
    lbs_f = jnp.cumsum(jax.nn.softmax(lb_fwd.astype(jnp.float32), axis=0), axis=0)
    lbs_b = jnp.cumsum(jax.nn.softmax(lb_bwd.astype(jnp.float32), axis=0), axis=0)
    for l in range(DEPTH):
        x = encoder_layer(x, norm_mix_w[l], w_in[l], lbs_f[l], lbs_b[l], hg_norm_w[l],
                          q_norm_w[l], k_norm_w[l], w_branch_a[l], w_branch_b[l], w_out[l],
                          norm_ffn_w[l], w_gate_up[l], w_down[l])
    return rmsnorm(x, norm_final_w)


def setup_inputs(seed: int = 0) -> dict:
    key = jax.random.key(seed)
    ks = jax.random.split(key, 20)
    nrm = lambda k, shape, s: jax.random.normal(k, shape, jnp.float32) * s
    gain = lambda k, shape: 1.0 + 0.02 * jax.random.normal(k, shape, jnp.float32)
    return {
        'x_prompt': nrm(ks[0], (BATCH, SEQ, D_MODEL), 1.0),
        'x_sample': nrm(ks[1], (DEC_BATCH, DEC_SEQ, D_MODEL), 1.0),
        'norm_mix_w': gain(ks[2], (DEPTH, D_MODEL)),
        'w_in': nrm(ks[3], (DEPTH, D_MODEL, IN_WIDTH), D_MODEL ** -0.5),
        'lb_fwd': nrm(ks[4], (DEPTH + 1, HG_WIDTH), 0.5),
        'lb_bwd': nrm(ks[5], (DEPTH + 1, HG_WIDTH), 0.5),
        'hg_norm_w': gain(ks[6], (DEPTH, HG_HEADS * HG_DV)),
        'q_norm_w': gain(ks[7], (DEPTH, HEAD_DIM)),
        'k_norm_w': gain(ks[8], (DEPTH, HEAD_DIM)),
        'w_branch_a': nrm(ks[9], (DEPTH, HG_HEADS * HG_DV, D_MODEL), (HG_HEADS * HG_DV) ** -0.5),
        'w_branch_b': nrm(ks[10], (DEPTH, ATT_WIDTH, D_MODEL), ATT_WIDTH ** -0.5),
        'w_out': nrm(ks[11], (DEPTH, D_MODEL, D_MODEL), D_MODEL ** -0.5),
        'norm_ffn_w': gain(ks[12], (DEPTH, D_MODEL)),
        'w_gate_up': nrm(ks[13], (DEPTH, D_MODEL, 2 * D_FF), D_MODEL ** -0.5),
        'w_down': nrm(ks[14], (DEPTH, D_FF, D_MODEL), D_FF ** -0.5),
        'norm_final_w': gain(ks[15], (D_MODEL,)),
    }


def reference(x_prompt, x_sample, norm_mix_w, w_in, lb_fwd, lb_bwd, hg_norm_w, q_norm_w,
              k_norm_w, w_branch_a, w_branch_b, w_out, norm_ffn_w, w_gate_up, w_down,
              norm_final_w):
    y_prompt = trunk(x_prompt, norm_mix_w, w_in, lb_fwd, lb_bwd, hg_norm_w, q_norm_w, k_norm_w,
                     w_branch_a, w_branch_b, w_out, norm_ffn_w, w_gate_up, w_down, norm_final_w)
    y_sample = trunk(x_sample, norm_mix_w, w_in, lb_fwd, lb_bwd, hg_norm_w, q_norm_w, k_norm_w,
                     w_branch_a, w_branch_b, w_out, norm_ffn_w, w_gate_up, w_down, norm_final_w)
    return (y_prompt, y_sample)
```

```python
import functools

import jax
import jax.numpy as jnp
import numpy as np
from jax import lax
from jax.experimental import pallas as pl
from jax.experimental.pallas import tpu as pltpu

F32 = jnp.float32
BF16 = jnp.bfloat16

EPS = 1e-6
HEAD = 128
HG_HEADS = 16
ATT_Q_HEADS = 16
ATT_KV_HEADS = 4
GRID_W = 64
ROPE_THETA = 10000.0

V7X_VMEM_LIMIT_BYTES = 56 * 1024 * 1024

HG_CHUNK = 64
HG_TILE = 256
HG_MAX_EXPONENT = 80.0


def _params(*semantics):
    return pltpu.CompilerParams(dimension_semantics=semantics,
                                vmem_limit_bytes=V7X_VMEM_LIMIT_BYTES)


def _rmsnorm_kernel(x_ref, w_ref, o_ref):
    x = x_ref[...]
    inv = lax.rsqrt(jnp.mean(x * x, axis=-1, keepdims=True) + EPS)
    o_ref[...] = (x * inv * w_ref[...]).astype(o_ref.dtype)


def rmsnorm(x, w, out_dtype, tm=256):
    T, D = x.shape
    return pl.pallas_call(
        _rmsnorm_kernel,
        grid=(T // tm,),
        in_specs=[pl.BlockSpec((tm, D), lambda i: (i, 0)),
                  pl.BlockSpec((1, D), lambda i: (0, 0))],
        out_specs=pl.BlockSpec((tm, D), lambda i: (i, 0)),
        out_shape=jax.ShapeDtypeStruct((T, D), out_dtype),
        compiler_params=_params("parallel"),
        name="rmsnorm",
    )(x, w.reshape(1, D))


def _proj_kernel(a_ref, w_ref, o_ref):
    o_ref[...] = jnp.dot(a_ref[...], w_ref[...], preferred_element_type=F32)


def in_projection(h, w, tm=1024, tn=1024):
    T, K = h.shape
    N = w.shape[1]
    return pl.pallas_call(
        _proj_kernel,
        grid=(T // tm, N // tn),
        in_specs=[pl.BlockSpec((tm, K), lambda i, j: (i, 0)),
                  pl.BlockSpec((K, tn), lambda i, j: (0, j))],
        out_specs=pl.BlockSpec((tm, tn), lambda i, j: (i, j)),
        out_shape=jax.ShapeDtypeStruct((T, N), F32),
        compiler_params=_params("parallel", "arbitrary"),
        name="in_projection",
    )(h, w)


def _merge_kernel(oa_ref, ob_ref, wa_ref, wb_ref, ga_ref, gb_ref, o_ref):
    ya = jnp.dot(oa_ref[...], wa_ref[...], preferred_element_type=F32)
    yb = jnp.dot(ob_ref[...], wb_ref[...], preferred_element_type=F32)
    o_ref[...] = (jax.nn.sigmoid(ga_ref[...]) * ya
                  + jax.nn.sigmoid(gb_ref[...]) * yb).astype(o_ref.dtype)


def gated_merge(o_hg, o_att, w_a, w_b, proj, ga_col, gb_col, tm=1024, tn=512):
    T, K = o_hg.shape
    D = w_a.shape[1]
    ga_blk, gb_blk = ga_col // tn, gb_col // tn
    return pl.pallas_call(
        _merge_kernel,
        grid=(T // tm, D // tn),
        in_specs=[pl.BlockSpec((tm, K), lambda i, j: (i, 0)),
                  pl.BlockSpec((tm, K), lambda i, j: (i, 0)),
                  pl.BlockSpec((K, tn), lambda i, j: (0, j)),
                  pl.BlockSpec((K, tn), lambda i, j: (0, j)),
                  pl.BlockSpec((tm, tn), lambda i, j: (i, ga_blk + j)),
                  pl.BlockSpec((tm, tn), lambda i, j: (i, gb_blk + j))],
        out_specs=pl.BlockSpec((tm, tn), lambda i, j: (i, j)),
        out_shape=jax.ShapeDtypeStruct((T, D), BF16),
        compiler_params=_params("parallel", "arbitrary"),
        name="gated_merge",
    )(o_hg, o_att, w_a, w_b, proj, proj)


def _residual_mm_kernel(a_ref, w_ref, r_ref, o_ref):
    o_ref[...] = r_ref[...] + jnp.dot(a_ref[...], w_ref[...], preferred_element_type=F32)


def residual_matmul(a, w, res, tm=1024, tn=512):
    T, K = a.shape
    N = w.shape[1]
    return pl.pallas_call(
        _residual_mm_kernel,
        grid=(T // tm, N // tn),
        in_specs=[pl.BlockSpec((tm, K), lambda i, j: (i, 0)),
                  pl.BlockSpec((K, tn), lambda i, j: (0, j)),
                  pl.BlockSpec((tm, tn), lambda i, j: (i, j))],
        out_specs=pl.BlockSpec((tm, tn), lambda i, j: (i, j)),
        out_shape=jax.ShapeDtypeStruct((T, N), F32),
        compiler_params=_params("parallel", "arbitrary"),
        name="residual_matmul",
    )(a, w, res)


def _swiglu_up_kernel(a_ref, wg_ref, wu_ref, o_ref):
    a = a_ref[...]
    g = jnp.dot(a, wg_ref[...], preferred_element_type=F32)
    u = jnp.dot(a, wu_ref[...], preferred_element_type=F32)
    o_ref[...] = (jax.nn.silu(g) * u).astype(o_ref.dtype)


def swiglu_up(h, w_gate_up, tm=1024, tn=256):
    T, K = h.shape
    F = w_gate_up.shape[1] // 2
    u_blk = F // tn
    return pl.pallas_call(
        _swiglu_up_kernel,
        grid=(T // tm, F // tn),
        in_specs=[pl.BlockSpec((tm, K), lambda i, j: (i, 0)),
                  pl.BlockSpec((K, tn), lambda i, j: (0, j)),
                  pl.BlockSpec((K, tn), lambda i, j: (0, u_blk + j))],
        out_specs=pl.BlockSpec((tm, tn), lambda i, j: (i, j)),
        out_shape=jax.ShapeDtypeStruct((T, F), BF16),
        compiler_params=_params("parallel", "arbitrary"),
        name="swiglu_up",
    )(h, w_gate_up, w_gate_up)


def _down_kernel(a_ref, w_ref, r_ref, o_ref):
    part = jnp.dot(a_ref[...], w_ref[...], preferred_element_type=F32)

    @pl.when(pl.program_id(2) == 0)
    def _():
        o_ref[...] = r_ref[...] + part

    @pl.when(pl.program_id(2) != 0)
    def _():
        o_ref[...] += part


def swiglu_down(act, w, res, tm=1024, tn=512, k_splits=2):
    T, K = act.shape
    N = w.shape[1]
    tk = K // k_splits
    return pl.pallas_call(
        _down_kernel,
        grid=(T // tm, N // tn, k_splits),
        in_specs=[pl.BlockSpec((tm, tk), lambda i, j, k: (i, k)),
                  pl.BlockSpec((tk, tn), lambda i, j, k: (k, j)),
                  pl.BlockSpec((tm, tn), lambda i, j, k: (i, j))],
        out_specs=pl.BlockSpec((tm, tn), lambda i, j, k: (i, j)),
        out_shape=jax.ShapeDtypeStruct((T, N), F32),
        compiler_params=_params("parallel", "arbitrary", "arbitrary"),
        name="swiglu_down",
    )(act, w, res)


def _split3_bf16(x):
    hi = x.astype(BF16)
    r1 = x - hi.astype(F32)
    mid = r1.astype(BF16)
    lo = (r1 - mid.astype(F32)).astype(BF16)
    return hi, mid, lo


def _chunk_cumsum(tri_bf16, logf):
    hi, mid, lo = _split3_bf16(logf)
    dot = functools.partial(jnp.dot, preferred_element_type=F32)
    return dot(tri_bf16, hi) + dot(tri_bf16, mid) + dot(tri_bf16, lo)


def _forget_gate(z, lb):
    e = jnp.exp(-jnp.abs(z))
    r = 1.0 / (1.0 + e)
    pos = z >= 0
    sig = jnp.where(pos, r, e * r)
    sig_neg = jnp.where(pos, e * r, r)
    f = lb + (1.0 - lb) * sig
    return f, jnp.log(f), (1.0 - lb) * sig_neg


def _hgrn_scan_tile(q, z, v, lb, st_ref, o_scr, vt_scr, reverse):
    C = HG_CHUNK
    n_chunks = HG_TILE // C
    row = lax.broadcasted_iota(jnp.int32, (C, C), 0)
    col = lax.broadcasted_iota(jnp.int32, (C, C), 1)
    keep = (col >= row) if reverse else (col <= row)
    tri = jnp.where(keep, 1.0, 0.0).astype(BF16)
    end = 0 if reverse else C - 1
    mid = C // 2 if reverse else C // 2 - 1

    f, logf, kk = _forget_gate(z, lb)

    cums, worst = [], None
    for j in range(n_chunks):
        c = _chunk_cumsum(tri, logf[j * C:(j + 1) * C])
        cums.append(c)
        m = jnp.max(jnp.abs(c - c[mid:mid + 1]))
        worst = m if worst is None else jnp.maximum(worst, m)
    safe = worst < HG_MAX_EXPONENT

    @pl.when(safe)
    def _():
        order = range(n_chunks - 1, -1, -1) if reverse else range(n_chunks)
        for j in order:
            sl = slice(j * C, (j + 1) * C)
            c, qc, kc = cums[j], q[sl], kk[sl]
            vc = v[sl].astype(BF16)
            a = c - c[mid:mid + 1]
            c_end = c[end:end + 1]
            q_in = (qc * jnp.exp(a)).astype(BF16)
            k_in = (kc * jnp.exp(-a)).astype(BF16)
            s = lax.dot_general(q_in, k_in, (((1,), (1,)), ((), ())),
                                preferred_element_type=F32)
            s = jnp.where(keep, s, 0.0).astype(BF16)
            q_st = (qc * jnp.exp(c)).astype(BF16)
            st = st_ref[...]
            o = lax.dot_general(q_st, st.astype(BF16), (((1,), (1,)), ((), ())),
                                preferred_element_type=F32)
            o_scr[sl, :] = o + jnp.dot(s, vc, preferred_element_type=F32)
            k_st = (kc * jnp.exp(c_end - c)).astype(BF16)
            vt = v[sl].T.astype(BF16)
            st_ref[...] = st * jnp.exp(c_end) + jnp.dot(vt, k_st, preferred_element_type=F32)

    @pl.when(jnp.logical_not(safe))
    def _():
        vt_scr[...] = v.T
        lane = lax.broadcasted_iota(jnp.int32, (1, HG_TILE), 1)
        qk_scr = o_scr

        def body(i, carry):
            t = (HG_TILE - 1 - i) if reverse else i
            onehot = (lane == t).astype(F32)
            rows = lax.broadcasted_iota(jnp.int32, (HG_TILE, 1), 0) == t
            pick = lambda x: jnp.sum(jnp.where(rows, x, 0.0), axis=0, keepdims=True)
            f_t, k_t, q_t = pick(f), pick(kk), pick(q)
            v_col = jnp.sum(vt_scr[...] * onehot, axis=1, keepdims=True)
            st = st_ref[...] * f_t + v_col * k_t
            st_ref[...] = st
            o_col = jnp.sum(st * q_t, axis=1, keepdims=True)
            return carry + o_col * onehot

        ot = lax.fori_loop(0, HG_TILE, body, jnp.zeros((HEAD, HG_TILE), F32))
        qk_scr[...] = ot.T


def _hgrn_fwd_kernel(q_ref, z_ref, v_ref, lb_ref, o_ref, st_ref, o_scr, vt_scr):
    @pl.when(pl.program_id(1) == 0)
    def _():
        st_ref[...] = jnp.zeros_like(st_ref)

    _hgrn_scan_tile(q_ref[...], z_ref[...], v_ref[...], lb_ref[...],
                    st_ref, o_scr, vt_scr, reverse=False)
    o_ref[...] = o_scr[...]


def _hgrn_bwd_kernel(q_ref, z_ref, v_ref, lb_ref, of_ref, gate_ref, nw_ref, o_ref,
                     st_ref, o_scr, vt_scr):
    @pl.when(pl.program_id(1) == 0)
    def _():
        st_ref[...] = jnp.zeros_like(st_ref)

    _hgrn_scan_tile(q_ref[...], z_ref[...], v_ref[...], lb_ref[...],
                    st_ref, o_scr, vt_scr, reverse=True)
    o = of_ref[...] + o_scr[...]
    o = o * lax.rsqrt(jnp.mean(o * o, axis=-1, keepdims=True) + EPS) * nw_ref[...]
    o_ref[...] = (o * jax.nn.silu(gate_ref[...])).astype(o_ref.dtype)


def hgrn2_bidirectional(proj, lb_f, lb_b, norm_w, cols):
    T = proj.shape[0]
    n_tiles = T // HG_TILE
    W = HG_HEADS * HEAD
    cq, czf, czb, ci, cg = (c // HEAD for c in cols)
    scratch = [pltpu.VMEM((HEAD, HEAD), F32),
               pltpu.VMEM((HG_TILE, HEAD), F32),
               pltpu.VMEM((HEAD, HG_TILE), F32)]
    tile = lambda blk: pl.BlockSpec((HG_TILE, HEAD), lambda h, i: (i, blk + h))
    vec = pl.BlockSpec((1, HEAD), lambda h, i: (0, h))
    o_fwd = pl.pallas_call(
        _hgrn_fwd_kernel,
        grid=(HG_HEADS, n_tiles),
        in_specs=[tile(cq), tile(czf), tile(ci), vec],
        out_specs=pl.BlockSpec((HG_TILE, HEAD), lambda h, i: (i, h)),
        out_shape=jax.ShapeDtypeStruct((T, W), F32),
        scratch_shapes=scratch,
        compiler_params=_params("parallel", "arbitrary"),
        name="hgrn_fwd",
    )(proj, proj, proj, lb_f.reshape(1, W))

    last = n_tiles - 1
    rtile = lambda blk: pl.BlockSpec((HG_TILE, HEAD), lambda h, i: (last - i, blk + h))
    return pl.pallas_call(
        _hgrn_bwd_kernel,
        grid=(HG_HEADS, n_tiles),
        in_specs=[rtile(cq), rtile(czb), rtile(ci), vec, rtile(0), rtile(cg), vec],
        out_specs=pl.BlockSpec((HG_TILE, HEAD), lambda h, i: (last - i, h)),
        out_shape=jax.ShapeDtypeStruct((T, W), BF16),
        scratch_shapes=scratch,
        compiler_params=_params("parallel", "arbitrary"),
        name="hgrn_bwd",
    )(proj, proj, proj, lb_b.reshape(1, W), o_fwd, proj, norm_w.reshape(1, W))


def _rope_tables(T):
    rows = T // GRID_W
    row = jnp.repeat(jnp.arange(rows, dtype=F32), GRID_W)
    col = jnp.tile(jnp.arange(GRID_W, dtype=F32), rows)
    axis_dim = HEAD // 2
    inv = ROPE_THETA ** (-jnp.arange(0, axis_dim, 2, dtype=F32) / axis_dim)
    ang = jnp.concatenate([row[:, None] * inv, col[:, None] * inv], axis=-1)
    cos = jnp.repeat(jnp.cos(ang), 2, axis=-1)
    sin = jnp.repeat(jnp.sin(ang), 2, axis=-1)
    sign = jnp.tile(jnp.array([-1.0, 1.0], F32), HEAD // 2)
    return cos, sin * sign


def _norm_rope_head(x, w, cos, sin_signed, scale):
    x = x * lax.rsqrt(jnp.mean(x * x, axis=-1, keepdims=True) + EPS) * w
    lane = lax.broadcasted_iota(jnp.int32, x.shape, 1)
    partner = jnp.where(lane % 2 == 0,
                        pltpu.roll(x, HEAD - 1, axis=1),
                        pltpu.roll(x, 1, axis=1))
    out = x * cos + partner * sin_signed
    return out * scale if scale is not None else out


def _qk_rope_kernel(q_ref, k_ref, v_ref, qw_ref, kw_ref, cos_ref, sin_ref,
                    qo_ref, ko_ref, vo_ref, *, q_scale):
    cos, sin = cos_ref[...], sin_ref[...]
    for h in range(ATT_Q_HEADS):
        sl = slice(h * HEAD, (h + 1) * HEAD)
        qo_ref[:, sl] = _norm_rope_head(q_ref[:, sl], qw_ref[...], cos, sin,
                                        q_scale).astype(qo_ref.dtype)
    for h in range(ATT_KV_HEADS):
        sl = slice(h * HEAD, (h + 1) * HEAD)
        ko_ref[:, sl] = _norm_rope_head(k_ref[:, sl], kw_ref[...], cos, sin,
                                        None).astype(ko_ref.dtype)
    vo_ref[...] = v_ref[...].astype(vo_ref.dtype)


def qk_norm_rope(proj, q_w, k_w, cols, tm=256):
    T = proj.shape[0]
    QW, KW = ATT_Q_HEADS * HEAD, ATT_KV_HEADS * HEAD
    cq, ck, cv = cols
    cos, sin = _rope_tables(T)
    return pl.pallas_call(
        functools.partial(_qk_rope_kernel, q_scale=float(1.0 / np.sqrt(np.float32(HEAD)))),
        grid=(T // tm,),
        in_specs=[pl.BlockSpec((tm, QW), lambda i: (i, cq // QW)),
                  pl.BlockSpec((tm, KW), lambda i: (i, ck // KW)),
                  pl.BlockSpec((tm, KW), lambda i: (i, cv // KW)),
                  pl.BlockSpec((1, HEAD), lambda i: (0, 0)),
                  pl.BlockSpec((1, HEAD), lambda i: (0, 0)),
                  pl.BlockSpec((tm, HEAD), lambda i: (i, 0)),
                  pl.BlockSpec((tm, HEAD), lambda i: (i, 0))],
        out_specs=[pl.BlockSpec((tm, QW), lambda i: (i, 0)),
                   pl.BlockSpec((tm, KW), lambda i: (i, 0)),
                   pl.BlockSpec((tm, KW), lambda i: (i, 0))],
        out_shape=[jax.ShapeDtypeStruct((T, QW), BF16),
                   jax.ShapeDtypeStruct((T, KW), BF16),
                   jax.ShapeDtypeStruct((T, KW), BF16)],
        compiler_params=_params("parallel"),
        name="qk_norm_rope",
    )(proj, proj, proj, q_w.reshape(1, HEAD), k_w.reshape(1, HEAD), cos, sin)


def _flash_kernel(q_ref, k_ref, v_ref, o_ref, m_scr, l_scr, acc_scr):
    j = pl.program_id(2)
    group = ATT_Q_HEADS // ATT_KV_HEADS

    @pl.when(j == 0)
    def _():
        m_scr[...] = jnp.full_like(m_scr, -jnp.inf)
        l_scr[...] = jnp.zeros_like(l_scr)
        acc_scr[...] = jnp.zeros_like(acc_scr)

    k = k_ref[...]
    v = v_ref[...]
    for r in range(group):
        sl = slice(r * HEAD, (r + 1) * HEAD)
        s = lax.dot_general(q_ref[:, sl], k, (((1,), (1,)), ((), ())),
                            preferred_element_type=F32)
        m_prev = m_scr[r]
        m_new = jnp.maximum(m_prev, jnp.max(s, axis=-1, keepdims=True))
        alpha = jnp.exp(m_prev - m_new)
        p = jnp.exp(s - m_new)
        l_scr[r] = alpha * l_scr[r] + jnp.sum(p, axis=-1, keepdims=True)
        acc_scr[r] = alpha * acc_scr[r] + jnp.dot(p.astype(BF16), v,
                                                  preferred_element_type=F32)
        m_scr[r] = m_new

    @pl.when(j == pl.num_programs(2) - 1)
    def _():
        for r in range(group):
            sl = slice(r * HEAD, (r + 1) * HEAD)
            o_ref[:, sl] = (acc_scr[r] / l_scr[r]).astype(o_ref.dtype)


def flash_attention(q, k, v, tq=512, tk=512):
    T = q.shape[0]
    group = ATT_Q_HEADS // ATT_KV_HEADS
    GW = group * HEAD
    return pl.pallas_call(
        _flash_kernel,
        grid=(ATT_KV_HEADS, T // tq, T // tk),
        in_specs=[pl.BlockSpec((tq, GW), lambda g, i, j: (i, g)),
                  pl.BlockSpec((tk, HEAD), lambda g, i, j: (j, g)),
                  pl.BlockSpec((tk, HEAD), lambda g, i, j: (j, g))],
        out_specs=pl.BlockSpec((tq, GW), lambda g, i, j: (i, g)),
        out_shape=jax.ShapeDtypeStruct((T, ATT_Q_HEADS * HEAD), BF16),
        scratch_shapes=[pltpu.VMEM((group, tq, 1), F32),
                        pltpu.VMEM((group, tq, 1), F32),
                        pltpu.VMEM((group, tq, HEAD), F32)],
        compiler_params=_params("parallel", "parallel", "arbitrary"),
        name="flash_attention",
    )(q, k, v)


def _encoder_layer(x, norm_mix_w, w_in, lb_f, lb_b, hg_norm_w, q_norm_w, k_norm_w,
                   w_branch_a, w_branch_b, w_out, norm_ffn_w, w_gate_up, w_down):
    hg_w = HG_HEADS * HEAD
    q_w, kv_w = ATT_Q_HEADS * HEAD, ATT_KV_HEADS * HEAD
    D = x.shape[1]
    sizes = (hg_w, hg_w, hg_w, hg_w, hg_w, q_w, kv_w, kv_w, D, D)
    offs = np.concatenate([[0], np.cumsum(sizes)]).tolist()
    assert offs[-1] == w_in.shape[1]

    h = rmsnorm(x, norm_mix_w, BF16)
    proj = in_projection(h, w_in)
    o_hg = hgrn2_bidirectional(proj, lb_f, lb_b, hg_norm_w, offs[0:5])
    q, k, v = qk_norm_rope(proj, q_norm_w, k_norm_w, offs[5:8])
    o_att = flash_attention(q, k, v)
    merged = gated_merge(o_hg, o_att, w_branch_a, w_branch_b, proj, offs[8], offs[9])
    x = residual_matmul(merged, w_out, x)
    h = rmsnorm(x, norm_ffn_w, BF16)
    act = swiglu_up(h, w_gate_up)
    return swiglu_down(act, w_down, x)


def kernel(x_prompt, x_sample, norm_mix_w, w_in, lb_fwd, lb_bwd, hg_norm_w, q_norm_w, k_norm_w,
           w_branch_a, w_branch_b, w_out, norm_ffn_w, w_gate_up, w_down, norm_final_w):
    depth = w_in.shape[0]
    lbs_f = jnp.cumsum(jax.nn.softmax(lb_fwd.astype(F32), axis=0), axis=0)
    lbs_b = jnp.cumsum(jax.nn.softmax(lb_bwd.astype(F32), axis=0), axis=0)
    bf = lambda w: w.astype(BF16)
    w_in, w_branch_a, w_branch_b, w_out, w_gate_up, w_down = map(
        bf, (w_in, w_branch_a, w_branch_b, w_out, w_gate_up, w_down))

    def trunk(x):
        x = x.reshape(x.shape[-2], x.shape[-1])
        for l in range(depth):
            x = _encoder_layer(x, norm_mix_w[l], w_in[l], lbs_f[l], lbs_b[l], hg_norm_w[l],
                               q_norm_w[l], k_norm_w[l], w_branch_a[l], w_branch_b[l], w_out[l],
                               norm_ffn_w[l], w_gate_up[l], w_down[l])
        return rmsnorm(x, norm_final_w, F32)

    assert x_prompt.shape[0] == 1 and x_sample.shape[0] == 1
    return trunk(x_prompt)[None], trunk(x_sample)[None]
```

```python
import functools

import jax
import jax.numpy as jnp
import numpy as np
from jax import lax
from jax.experimental import pallas as pl
from jax.experimental.pallas import tpu as pltpu

F32 = jnp.float32
BF16 = jnp.bfloat16

EPS = 1e-6
HEAD = 128
HG_HEADS = 16
ATT_Q_HEADS = 16
ATT_KV_HEADS = 4
GRID_W = 64
ROPE_THETA = 10000.0

V7X_VMEM_LIMIT_BYTES = 56 * 1024 * 1024

HG_CHUNK = 64
HG_TILE = 256
HG_GROUP = 4
HG_MAX_EXPONENT = 80.0

ATT_SUB_K = 512
ATT_MAX_BOUNDED_LOGIT = 60.0


def _params(*semantics):
    return pltpu.CompilerParams(dimension_semantics=semantics,
                                vmem_limit_bytes=V7X_VMEM_LIMIT_BYTES)


def _rmsnorm_kernel(x_ref, w_ref, o_ref):
    x = x_ref[...]
    inv = lax.rsqrt(jnp.mean(x * x, axis=-1, keepdims=True) + EPS)
    o_ref[...] = (x * inv * w_ref[...]).astype(o_ref.dtype)


def rmsnorm(x, w, out_dtype, tm=256):
    T, D = x.shape
    return pl.pallas_call(
        _rmsnorm_kernel,
        grid=(T // tm,),
        in_specs=[pl.BlockSpec((tm, D), lambda i: (i, 0)),
                  pl.BlockSpec((1, D), lambda i: (0, 0))],
        out_specs=pl.BlockSpec((tm, D), lambda i: (i, 0)),
        out_shape=jax.ShapeDtypeStruct((T, D), out_dtype),
        compiler_params=_params("parallel"),
        name="rmsnorm",
    )(x, w.reshape(1, D))


def _proj_kernel(a_ref, w_ref, o_ref):
    o_ref[...] = jnp.dot(a_ref[...], w_ref[...], preferred_element_type=F32)


def in_projection(h, w, tm=1024, tn=1024):
    T, K = h.shape
    N = w.shape[1]
    return pl.pallas_call(
        _proj_kernel,
        grid=(T // tm, N // tn),
        in_specs=[pl.BlockSpec((tm, K), lambda i, j: (i, 0)),
                  pl.BlockSpec((K, tn), lambda i, j: (0, j))],
        out_specs=pl.BlockSpec((tm, tn), lambda i, j: (i, j)),
        out_shape=jax.ShapeDtypeStruct((T, N), F32),
        compiler_params=_params("parallel", "arbitrary"),
        name="in_projection",
    )(h, w)


def _merge_kernel(oa_ref, ob_ref, wa_ref, wb_ref, ga_ref, gb_ref, o_ref):
    ya = jnp.dot(oa_ref[...], wa_ref[...], preferred_element_type=F32)
    yb = jnp.dot(ob_ref[...], wb_ref[...], preferred_element_type=F32)
    o_ref[...] = (jax.nn.sigmoid(ga_ref[...]) * ya
                  + jax.nn.sigmoid(gb_ref[...]) * yb).astype(o_ref.dtype)


def gated_merge(o_hg, o_att, w_a, w_b, proj, ga_col, gb_col, tm=1024, tn=512):
    T, K = o_hg.shape
    D = w_a.shape[1]
    ga_blk, gb_blk = ga_col // tn, gb_col // tn
    return pl.pallas_call(
        _merge_kernel,
        grid=(T // tm, D // tn),
        in_specs=[pl.BlockSpec((tm, K), lambda i, j: (i, 0)),
                  pl.BlockSpec((tm, K), lambda i, j: (i, 0)),
                  pl.BlockSpec((K, tn), lambda i, j: (0, j)),
                  pl.BlockSpec((K, tn), lambda i, j: (0, j)),
                  pl.BlockSpec((tm, tn), lambda i, j: (i, ga_blk + j)),
                  pl.BlockSpec((tm, tn), lambda i, j: (i, gb_blk + j))],
        out_specs=pl.BlockSpec((tm, tn), lambda i, j: (i, j)),
        out_shape=jax.ShapeDtypeStruct((T, D), BF16),
        compiler_params=_params("parallel", "arbitrary"),
        name="gated_merge",
    )(o_hg, o_att, w_a, w_b, proj, proj)


def _residual_mm_kernel(a_ref, w_ref, r_ref, o_ref):
    o_ref[...] = r_ref[...] + jnp.dot(a_ref[...], w_ref[...], preferred_element_type=F32)


def residual_matmul(a, w, res, tm=1024, tn=512):
    T, K = a.shape
    N = w.shape[1]
    return pl.pallas_call(
        _residual_mm_kernel,
        grid=(T // tm, N // tn),
        in_specs=[pl.BlockSpec((tm, K), lambda i, j: (i, 0)),
                  pl.BlockSpec((K, tn), lambda i, j: (0, j)),
                  pl.BlockSpec((tm, tn), lambda i, j: (i, j))],
        out_specs=pl.BlockSpec((tm, tn), lambda i, j: (i, j)),
        out_shape=jax.ShapeDtypeStruct((T, N), F32),
        compiler_params=_params("parallel", "arbitrary"),
        name="residual_matmul",
    )(a, w, res)


def _swiglu_up_kernel(a_ref, wg_ref, wu_ref, o_ref):
    a = a_ref[...]
    g = jnp.dot(a, wg_ref[...], preferred_element_type=F32)
    u = jnp.dot(a, wu_ref[...], preferred_element_type=F32)
    o_ref[...] = (jax.nn.silu(g) * u).astype(o_ref.dtype)


def swiglu_up(h, w_gate_up, tm=1024, tn=256):
    T, K = h.shape
    F = w_gate_up.shape[1] // 2
    u_blk = F // tn
    return pl.pallas_call(
        _swiglu_up_kernel,
        grid=(T // tm, F // tn),
        in_specs=[pl.BlockSpec((tm, K), lambda i, j: (i, 0)),
                  pl.BlockSpec((K, tn), lambda i, j: (0, j)),
                  pl.BlockSpec((K, tn), lambda i, j: (0, u_blk + j))],
        out_specs=pl.BlockSpec((tm, tn), lambda i, j: (i, j)),
        out_shape=jax.ShapeDtypeStruct((T, F), BF16),
        compiler_params=_params("parallel", "arbitrary"),
        name="swiglu_up",
    )(h, w_gate_up, w_gate_up)


def _down_kernel(a_ref, w_ref, r_ref, o_ref):
    part = jnp.dot(a_ref[...], w_ref[...], preferred_element_type=F32)

    @pl.when(pl.program_id(2) == 0)
    def _():
        o_ref[...] = r_ref[...] + part

    @pl.when(pl.program_id(2) != 0)
    def _():
        o_ref[...] += part


def swiglu_down(act, w, res, tm=1024, tn=512, k_splits=2):
    T, K = act.shape
    N = w.shape[1]
    tk = K // k_splits
    return pl.pallas_call(
        _down_kernel,
        grid=(T // tm, N // tn, k_splits),
        in_specs=[pl.BlockSpec((tm, tk), lambda i, j, k: (i, k)),
                  pl.BlockSpec((tk, tn), lambda i, j, k: (k, j)),
                  pl.BlockSpec((tm, tn), lambda i, j, k: (i, j))],
        out_specs=pl.BlockSpec((tm, tn), lambda i, j, k: (i, j)),
        out_shape=jax.ShapeDtypeStruct((T, N), F32),
        compiler_params=_params("parallel", "arbitrary", "arbitrary"),
        name="swiglu_down",
    )(act, w, res)


def _split3_bf16(x):
    hi = x.astype(BF16)
    r1 = x - hi.astype(F32)
    mid = r1.astype(BF16)
    lo = (r1 - mid.astype(F32)).astype(BF16)
    return hi, mid, lo


def _chunk_cumsum(tri_bf16, logf):
    hi, mid, lo = _split3_bf16(logf)
    dot = functools.partial(jnp.dot, preferred_element_type=F32)
    return dot(tri_bf16, hi) + dot(tri_bf16, mid) + dot(tri_bf16, lo)


def _forget_gate(z, lb):
    e = jnp.exp(-jnp.abs(z))
    r = 1.0 / (1.0 + e)
    pos = z >= 0
    sig = jnp.where(pos, r, e * r)
    sig_neg = jnp.where(pos, e * r, r)
    f = lb + (1.0 - lb) * sig
    return f, jnp.log(f), (1.0 - lb) * sig_neg


def _hgrn_scan_tile(q_ref, z_ref, v_ref, lb_ref, st_ref, o_scr, vt_scr, reverse):
    C = HG_CHUNK
    n_chunks = HG_TILE // C
    row = lax.broadcasted_iota(jnp.int32, (C, C), 0)
    col = lax.broadcasted_iota(jnp.int32, (C, C), 1)
    keep = (col >= row) if reverse else (col <= row)
    tri = jnp.where(keep, 1.0, 0.0).astype(BF16)
    end = 0 if reverse else C - 1
    mid = C // 2 if reverse else C // 2 - 1
    heads = [slice(g * HEAD, (g + 1) * HEAD) for g in range(HG_GROUP)]

    gates, cums, worst = [], {}, None
    for g, hs in enumerate(heads):
        f, logf, kk = _forget_gate(z_ref[:, hs], lb_ref[:, hs])
        gates.append((f, kk))
        for j in range(n_chunks):
            c = _chunk_cumsum(tri, logf[j * C:(j + 1) * C])
            cums[g, j] = c
            m = jnp.max(jnp.abs(c - c[mid:mid + 1]))
            worst = m if worst is None else jnp.maximum(worst, m)
    safe = worst < HG_MAX_EXPONENT

    @pl.when(safe)
    def _():
        order = range(n_chunks - 1, -1, -1) if reverse else range(n_chunks)
        for j in order:
            sl = slice(j * C, (j + 1) * C)
            for g, hs in enumerate(heads):
                c, qc, kc, vc = cums[g, j], q_ref[sl, hs], gates[g][1][sl], v_ref[sl, hs]
                a = c - c[mid:mid + 1]
                c_end = c[end:end + 1]
                q_in = (qc * jnp.exp(a)).astype(BF16)
                k_in = (kc * jnp.exp(-a)).astype(BF16)
                s = lax.dot_general(q_in, k_in, (((1,), (1,)), ((), ())),
                                    preferred_element_type=F32)
                s = jnp.where(keep, s, 0.0).astype(BF16)
                q_st = (qc * jnp.exp(c)).astype(BF16)
                st = st_ref[g]
                o = lax.dot_general(q_st, st.astype(BF16), (((1,), (1,)), ((), ())),
                                    preferred_element_type=F32)
                o_scr[sl, hs] = o + jnp.dot(s, vc.astype(BF16), preferred_element_type=F32)
                k_st = (kc * jnp.exp(c_end - c)).astype(BF16)
                st_ref[g] = st * jnp.exp(c_end) + jnp.dot(vc.T.astype(BF16), k_st,
                                                          preferred_element_type=F32)

    @pl.when(jnp.logical_not(safe))
    def _():
        lane = lax.broadcasted_iota(jnp.int32, (1, HG_TILE), 1)
        rows = lax.broadcasted_iota(jnp.int32, (HG_TILE, 1), 0)
        for g, hs in enumerate(heads):
            f, kk = gates[g]
            q = q_ref[:, hs]
            vt_scr[...] = v_ref[:, hs].T

            def body(i, carry):
                t = (HG_TILE - 1 - i) if reverse else i
                onehot = (lane == t).astype(F32)
                pick = lambda x: jnp.sum(jnp.where(rows == t, x, 0.0), axis=0, keepdims=True)
                f_t, k_t, q_t = pick(f), pick(kk), pick(q)
                v_col = jnp.sum(vt_scr[...] * onehot, axis=1, keepdims=True)
                st = st_ref[g] * f_t + v_col * k_t
                st_ref[g] = st
                o_col = jnp.sum(st * q_t, axis=1, keepdims=True)
                return carry + o_col * onehot

            ot = lax.fori_loop(0, HG_TILE, body, jnp.zeros((HEAD, HG_TILE), F32))
            o_scr[:, hs] = ot.T


def _hgrn_fwd_kernel(q_ref, z_ref, v_ref, lb_ref, o_ref, st_ref, o_scr, vt_scr):
    @pl.when(pl.program_id(1) == 0)
    def _():
        st_ref[...] = jnp.zeros_like(st_ref)

    _hgrn_scan_tile(q_ref, z_ref, v_ref, lb_ref, st_ref, o_scr, vt_scr, reverse=False)
    o_ref[...] = o_scr[...]


def _hgrn_bwd_kernel(q_ref, z_ref, v_ref, lb_ref, of_ref, gate_ref, nw_ref, o_ref,
                     st_ref, o_scr, vt_scr):
    @pl.when(pl.program_id(1) == 0)
    def _():
        st_ref[...] = jnp.zeros_like(st_ref)

    _hgrn_scan_tile(q_ref, z_ref, v_ref, lb_ref, st_ref, o_scr, vt_scr, reverse=True)
    for g in range(HG_GROUP):
        hs = slice(g * HEAD, (g + 1) * HEAD)
        o = of_ref[:, hs] + o_scr[:, hs]
        o = o * lax.rsqrt(jnp.mean(o * o, axis=-1, keepdims=True) + EPS) * nw_ref[:, hs]
        o_ref[:, hs] = (o * jax.nn.silu(gate_ref[:, hs])).astype(o_ref.dtype)


def hgrn2_bidirectional(proj, lb_f, lb_b, norm_w, cols):
    T = proj.shape[0]
    n_tiles = T // HG_TILE
    W = HG_HEADS * HEAD
    GW = HG_GROUP * HEAD
    n_groups = HG_HEADS // HG_GROUP
    cq, czf, czb, ci, cg = (c // GW for c in cols)
    scratch = [pltpu.VMEM((HG_GROUP, HEAD, HEAD), F32),
               pltpu.VMEM((HG_TILE, GW), F32),
               pltpu.VMEM((HEAD, HG_TILE), F32)]
    tile = lambda blk: pl.BlockSpec((HG_TILE, GW), lambda h, i: (i, blk + h))
    vec = pl.BlockSpec((1, GW), lambda h, i: (0, h))
    o_fwd = pl.pallas_call(
        _hgrn_fwd_kernel,
        grid=(n_groups, n_tiles),
        in_specs=[tile(cq), tile(czf), tile(ci), vec],
        out_specs=pl.BlockSpec((HG_TILE, GW), lambda h, i: (i, h)),
        out_shape=jax.ShapeDtypeStruct((T, W), F32),
        scratch_shapes=scratch,
        compiler_params=_params("parallel", "arbitrary"),
        name="hgrn_fwd",
    )(proj, proj, proj, lb_f.reshape(1, W))

    last = n_tiles - 1
    rtile = lambda blk: pl.BlockSpec((HG_TILE, GW), lambda h, i: (last - i, blk + h))
    return pl.pallas_call(
        _hgrn_bwd_kernel,
        grid=(n_groups, n_tiles),
        in_specs=[rtile(cq), rtile(czb), rtile(ci), vec, rtile(0), rtile(cg), vec],
        out_specs=pl.BlockSpec((HG_TILE, GW), lambda h, i: (last - i, h)),
        out_shape=jax.ShapeDtypeStruct((T, W), BF16),
        scratch_shapes=scratch,
        compiler_params=_params("parallel", "arbitrary"),
        name="hgrn_bwd",
    )(proj, proj, proj, lb_b.reshape(1, W), o_fwd, proj, norm_w.reshape(1, W))


def _rope_tables(T):
    rows = T // GRID_W
    row = jnp.repeat(jnp.arange(rows, dtype=F32), GRID_W)
    col = jnp.tile(jnp.arange(GRID_W, dtype=F32), rows)
    axis_dim = HEAD // 2
    inv = ROPE_THETA ** (-jnp.arange(0, axis_dim, 2, dtype=F32) / axis_dim)
    ang = jnp.concatenate([row[:, None] * inv, col[:, None] * inv], axis=-1)
    cos = jnp.repeat(jnp.cos(ang), 2, axis=-1)
    sin = jnp.repeat(jnp.sin(ang), 2, axis=-1)
    sign = jnp.tile(jnp.array([-1.0, 1.0], F32), HEAD // 2)
    return cos, sin * sign


def _norm_rope_head(x, w, cos, sin_signed, scale):
    x = x * lax.rsqrt(jnp.mean(x * x, axis=-1, keepdims=True) + EPS) * w
    lane = lax.broadcasted_iota(jnp.int32, x.shape, 1)
    partner = jnp.where(lane % 2 == 0,
                        pltpu.roll(x, HEAD - 1, axis=1),
                        pltpu.roll(x, 1, axis=1))
    out = x * cos + partner * sin_signed
    return out * scale if scale is not None else out


def _qk_rope_kernel(q_ref, k_ref, v_ref, qw_ref, kw_ref, cos_ref, sin_ref,
                    qt_ref, ko_ref, vt_ref, stat_ref, *, q_scale):
    cos, sin = cos_ref[...], sin_ref[...]

    def max_sq_norm(x_bf16):
        x = x_bf16.astype(F32)
        return jnp.max(jnp.sum(x * x, axis=-1, keepdims=True))

    q_sq = k_sq = None
    for h in range(ATT_Q_HEADS):
        sl = slice(h * HEAD, (h + 1) * HEAD)
        q = _norm_rope_head(q_ref[:, sl], qw_ref[...], cos, sin, q_scale).astype(qt_ref.dtype)
        qt_ref[sl, :] = q.astype(F32).T.astype(qt_ref.dtype)
        n = max_sq_norm(q)
        q_sq = n if q_sq is None else jnp.maximum(q_sq, n)
    for h in range(ATT_KV_HEADS):
        sl = slice(h * HEAD, (h + 1) * HEAD)
        k = _norm_rope_head(k_ref[:, sl], kw_ref[...], cos, sin, None).astype(ko_ref.dtype)
        ko_ref[:, sl] = k
        n = max_sq_norm(k)
        k_sq = n if k_sq is None else jnp.maximum(k_sq, n)
        vt_ref[0, sl, :] = v_ref[:, sl].T.astype(vt_ref.dtype)
    stat_ref[0, 0:1, :] = jnp.full((1, HEAD), q_sq, F32)
    stat_ref[0, 1:2, :] = jnp.full((1, HEAD), k_sq, F32)


def qk_norm_rope(proj, q_w, k_w, cols):
    T = proj.shape[0]
    tm = ATT_SUB_K
    QW, KW = ATT_Q_HEADS * HEAD, ATT_KV_HEADS * HEAD
    cq, ck, cv = cols
    cos, sin = _rope_tables(T)
    q_scale = float(np.log2(np.e) / np.sqrt(np.float32(HEAD)))
    qt, k, vt, stats = pl.pallas_call(
        functools.partial(_qk_rope_kernel, q_scale=q_scale),
        grid=(T // tm,),
        in_specs=[pl.BlockSpec((tm, QW), lambda i: (i, cq // QW)),
                  pl.BlockSpec((tm, KW), lambda i: (i, ck // KW)),
                  pl.BlockSpec((tm, KW), lambda i: (i, cv // KW)),
                  pl.BlockSpec((1, HEAD), lambda i: (0, 0)),
                  pl.BlockSpec((1, HEAD), lambda i: (0, 0)),
                  pl.BlockSpec((tm, HEAD), lambda i: (i, 0)),
                  pl.BlockSpec((tm, HEAD), lambda i: (i, 0))],
        out_specs=[pl.BlockSpec((QW, tm), lambda i: (0, i)),
                   pl.BlockSpec((tm, KW), lambda i: (i, 0)),
                   pl.BlockSpec((1, KW, tm), lambda i: (i, 0, 0)),
                   pl.BlockSpec((1, 2, HEAD), lambda i: (i, 0, 0))],
        out_shape=[jax.ShapeDtypeStruct((QW, T), BF16),
                   jax.ShapeDtypeStruct((T, KW), BF16),
                   jax.ShapeDtypeStruct((T // tm, KW, tm), BF16),
                   jax.ShapeDtypeStruct((T // tm, 2, HEAD), F32)],
        compiler_params=_params("parallel"),
        name="qk_norm_rope",
    )(proj, proj, proj, q_w.reshape(1, HEAD), k_w.reshape(1, HEAD), cos, sin)
    q_max = jnp.sqrt(jnp.max(stats[:, 0, 0]))
    k_max = jnp.sqrt(jnp.max(stats[:, 1, 0]))
    return qt, k, vt, q_max, k_max


def _flash_kernel(bounded_ref, qt_ref, k_ref, vt_ref, kmax_ref, o_ref, m_scr, l_scr, acc_scr):
    j = pl.program_id(2)
    group = ATT_Q_HEADS // ATT_KV_HEADS
    n_sub = k_ref.shape[0] // ATT_SUB_K
    bounded = bounded_ref[0] != 0

    def key_block(jj):
        k = k_ref[pl.ds(pl.multiple_of(jj * ATT_SUB_K, ATT_SUB_K), ATT_SUB_K), :]
        return k, vt_ref[jj]

    def logits(k, r):
        return jnp.dot(k, qt_ref[r * HEAD:(r + 1) * HEAD, :], preferred_element_type=F32)

    @pl.when(j == 0)
    def _():
        l_scr[...] = jnp.zeros_like(l_scr)
        acc_scr[...] = jnp.zeros_like(acc_scr)

        @pl.when(bounded)
        def _():
            for r in range(group):
                q = qt_ref[r * HEAD:(r + 1) * HEAD, :].astype(F32)
                m_scr[r] = jnp.sqrt(jnp.sum(q * q, axis=0, keepdims=True)) * kmax_ref[...]

        @pl.when(jnp.logical_not(bounded))
        def _():
            m_scr[...] = jnp.full_like(m_scr, -jnp.inf)

    @pl.when(bounded)
    def _():
        def sub_block(jj, carry):
            k, vt = key_block(jj)
            for r in range(group):
                pt = jnp.exp2(logits(k, r) - m_scr[r])
                l_scr[r] += jnp.sum(pt, axis=0, keepdims=True)
                acc_scr[r] += jnp.dot(vt, pt.astype(BF16), preferred_element_type=F32)
            return carry

        lax.fori_loop(0, n_sub, sub_block, 0, unroll=True)

    @pl.when(jnp.logical_not(bounded))
    def _():
        def sub_block(jj, carry):
            k, vt = key_block(jj)
            for r in range(group):
                st = logits(k, r)
                m_prev = m_scr[r]
                m_new = jnp.maximum(m_prev, jnp.max(st, axis=0, keepdims=True))
                alpha = jnp.exp2(m_prev - m_new)
                pt = jnp.exp2(st - m_new)
                l_scr[r] = alpha * l_scr[r] + jnp.sum(pt, axis=0, keepdims=True)
                acc_scr[r] = alpha * acc_scr[r] + jnp.dot(vt, pt.astype(BF16),
                                                          preferred_element_type=F32)
                m_scr[r] = m_new
            return carry

        lax.fori_loop(0, n_sub, sub_block, 0)

    @pl.when(j == pl.num_programs(2) - 1)
    def _():
        for r in range(group):
            o = acc_scr[r] / l_scr[r]
            o_ref[:, r * HEAD:(r + 1) * HEAD] = o.T.astype(o_ref.dtype)


def flash_attention(qt, k, vt, q_max, k_max, tq=512, tk=2048):
    T = k.shape[0]
    group = ATT_Q_HEADS // ATT_KV_HEADS
    GW = group * HEAD
    bounded = (q_max * k_max <= ATT_MAX_BOUNDED_LOGIT).astype(jnp.int32).reshape(1)
    kmax_row = jnp.full((1, tq), k_max, F32)
    grid_spec = pltpu.PrefetchScalarGridSpec(
        num_scalar_prefetch=1,
        grid=(ATT_KV_HEADS, T // tq, T // tk),
        in_specs=[pl.BlockSpec((GW, tq), lambda g, i, j, b: (g, i)),
                  pl.BlockSpec((tk, HEAD), lambda g, i, j, b: (j, g)),
                  pl.BlockSpec((tk // ATT_SUB_K, HEAD, ATT_SUB_K), lambda g, i, j, b: (j, g, 0)),
                  pl.BlockSpec((1, tq), lambda g, i, j, b: (0, 0))],
        out_specs=pl.BlockSpec((tq, GW), lambda g, i, j, b: (i, g)),
        scratch_shapes=[pltpu.VMEM((group, 1, tq), F32),
                        pltpu.VMEM((group, 1, tq), F32),
                        pltpu.VMEM((group, HEAD, tq), F32)])
    return pl.pallas_call(
        _flash_kernel,
        grid_spec=grid_spec,
        out_shape=jax.ShapeDtypeStruct((T, ATT_Q_HEADS * HEAD), BF16),
        compiler_params=_params("parallel", "parallel", "arbitrary"),
        name="flash_attention",
    )(bounded, qt, k, vt, kmax_row)


def _encoder_layer(x, norm_mix_w, w_in, lb_f, lb_b, hg_norm_w, q_norm_w, k_norm_w,
                   w_branch_a, w_branch_b, w_out, norm_ffn_w, w_gate_up, w_down):
    hg_w = HG_HEADS * HEAD
    q_w, kv_w = ATT_Q_HEADS * HEAD, ATT_KV_HEADS * HEAD
    D = x.shape[1]
    sizes = (hg_w, hg_w, hg_w, hg_w, hg_w, q_w, kv_w, kv_w, D, D)
    offs = np.concatenate([[0], np.cumsum(sizes)]).tolist()
    assert offs[-1] == w_in.shape[1]

    h = rmsnorm(x, norm_mix_w, BF16)
    proj = in_projection(h, w_in)
    o_hg = hgrn2_bidirectional(proj, lb_f, lb_b, hg_norm_w, offs[0:5])
    o_att = flash_attention(*qk_norm_rope(proj, q_norm_w, k_norm_w, offs[5:8]))
    merged = gated_merge(o_hg, o_att, w_branch_a, w_branch_b, proj, offs[8], offs[9])
    x = residual_matmul(merged, w_out, x)
    h = rmsnorm(x, norm_ffn_w, BF16)
    act = swiglu_up(h, w_gate_up)
    return swiglu_down(act, w_down, x)


def kernel(x_prompt, x_sample, norm_mix_w, w_in, lb_fwd, lb_bwd, hg_norm_w, q_norm_w, k_norm_w,
           w_branch_a, w_branch_b, w_out, norm_ffn_w, w_gate_up, w_down, norm_final_w):
    depth = w_in.shape[0]
    lbs_f = jnp.cumsum(jax.nn.softmax(lb_fwd.astype(F32), axis=0), axis=0)
    lbs_b = jnp.cumsum(jax.nn.softmax(lb_bwd.astype(F32), axis=0), axis=0)
    bf = lambda w: w.astype(BF16)
    w_in, w_branch_a, w_branch_b, w_out, w_gate_up, w_down = map(
        bf, (w_in, w_branch_a, w_branch_b, w_out, w_gate_up, w_down))

    def trunk(x):
        x = x.reshape(x.shape[-2], x.shape[-1])
        for l in range(depth):
            x = _encoder_layer(x, norm_mix_w[l], w_in[l], lbs_f[l], lbs_b[l], hg_norm_w[l],
                               q_norm_w[l], k_norm_w[l], w_branch_a[l], w_branch_b[l], w_out[l],
                               norm_ffn_w[l], w_gate_up[l], w_down[l])
        return rmsnorm(x, norm_final_w, F32)

    assert x_prompt.shape[0] == 1 and x_sample.shape[0] == 1
    return trunk(x_prompt)[None], trunk(x_sample)[None]
```

```python
import functools

import jax
import jax.numpy as jnp
import numpy as np
from jax import lax
from jax.experimental import pallas as pl
from jax.experimental.pallas import tpu as pltpu

F32 = jnp.float32
BF16 = jnp.bfloat16

EPS = 1e-6
HEAD = 128
HG_HEADS = 16
ATT_Q_HEADS = 16
ATT_KV_HEADS = 4
GRID_W = 64
ROPE_THETA = 10000.0

V7X_VMEM_LIMIT_BYTES = 56 * 1024 * 1024

HG_CHUNK = 64
HG_TILE = 256
HG_GROUP = 4
HG_MAX_EXPONENT = 80.0

ATT_SUB_K = 512
ATT_MAX_BOUNDED_LOGIT = 60.0


def _params(*semantics):
    return pltpu.CompilerParams(dimension_semantics=semantics,
                                vmem_limit_bytes=V7X_VMEM_LIMIT_BYTES)


def _rmsnorm_kernel(x_ref, w_ref, o_ref):
    x = x_ref[...]
    inv = lax.rsqrt(jnp.mean(x * x, axis=-1, keepdims=True) + EPS)
    o_ref[...] = (x * inv * w_ref[...]).astype(o_ref.dtype)


def rmsnorm(x, w, out_dtype, tm=256):
    T, D = x.shape
    return pl.pallas_call(
        _rmsnorm_kernel,
        grid=(T // tm,),
        in_specs=[pl.BlockSpec((tm, D), lambda i: (i, 0)),
                  pl.BlockSpec((1, D), lambda i: (0, 0))],
        out_specs=pl.BlockSpec((tm, D), lambda i: (i, 0)),
        out_shape=jax.ShapeDtypeStruct((T, D), out_dtype),
        compiler_params=_params("parallel"),
        name="rmsnorm",
    )(x, w.reshape(1, D))


def _proj_kernel(a_ref, w_ref, o_ref):
    o_ref[...] = jnp.dot(a_ref[...], w_ref[...], preferred_element_type=F32)


def in_projection(h, w, tm=1024, tn=1024):
    T, K = h.shape
    N = w.shape[1]
    return pl.pallas_call(
        _proj_kernel,
        grid=(T // tm, N // tn),
        in_specs=[pl.BlockSpec((tm, K), lambda i, j: (i, 0)),
                  pl.BlockSpec((K, tn), lambda i, j: (0, j))],
        out_specs=pl.BlockSpec((tm, tn), lambda i, j: (i, j)),
        out_shape=jax.ShapeDtypeStruct((T, N), F32),
        compiler_params=_params("parallel", "arbitrary"),
        name="in_projection",
    )(h, w)


def _merge_kernel(oa_ref, ob_ref, wa_ref, wb_ref, ga_ref, gb_ref, o_ref):
    ya = jnp.dot(oa_ref[...], wa_ref[...], preferred_element_type=F32)
    yb = jnp.dot(ob_ref[...], wb_ref[...], preferred_element_type=F32)
    o_ref[...] = (jax.nn.sigmoid(ga_ref[...]) * ya
                  + jax.nn.sigmoid(gb_ref[...]) * yb).astype(o_ref.dtype)


def gated_merge(o_hg, o_att, w_a, w_b, proj, ga_col, gb_col, tm=1024, tn=512):
    T, K = o_hg.shape
    D = w_a.shape[1]
    ga_blk, gb_blk = ga_col // tn, gb_col // tn
    return pl.pallas_call(
        _merge_kernel,
        grid=(T // tm, D // tn),
        in_specs=[pl.BlockSpec((tm, K), lambda i, j: (i, 0)),
                  pl.BlockSpec((tm, K), lambda i, j: (i, 0)),
                  pl.BlockSpec((K, tn), lambda i, j: (0, j)),
                  pl.BlockSpec((K, tn), lambda i, j: (0, j)),
                  pl.BlockSpec((tm, tn), lambda i, j: (i, ga_blk + j)),
                  pl.BlockSpec((tm, tn), lambda i, j: (i, gb_blk + j))],
        out_specs=pl.BlockSpec((tm, tn), lambda i, j: (i, j)),
        out_shape=jax.ShapeDtypeStruct((T, D), BF16),
        compiler_params=_params("parallel", "arbitrary"),
        name="gated_merge",
    )(o_hg, o_att, w_a, w_b, proj, proj)


def _residual_mm_kernel(a_ref, w_ref, r_ref, o_ref):
    o_ref[...] = r_ref[...] + jnp.dot(a_ref[...], w_ref[...], preferred_element_type=F32)


def residual_matmul(a, w, res, tm=1024, tn=512):
    T, K = a.shape
    N = w.shape[1]
    return pl.pallas_call(
        _residual_mm_kernel,
        grid=(T // tm, N // tn),
        in_specs=[pl.BlockSpec((tm, K), lambda i, j: (i, 0)),
                  pl.BlockSpec((K, tn), lambda i, j: (0, j)),
                  pl.BlockSpec((tm, tn), lambda i, j: (i, j))],
        out_specs=pl.BlockSpec((tm, tn), lambda i, j: (i, j)),
        out_shape=jax.ShapeDtypeStruct((T, N), F32),
        compiler_params=_params("parallel", "arbitrary"),
        name="residual_matmul",
    )(a, w, res)


def _swiglu_up_kernel(a_ref, wg_ref, wu_ref, o_ref):
    a = a_ref[...]
    g = jnp.dot(a, wg_ref[...], preferred_element_type=F32)
    u = jnp.dot(a, wu_ref[...], preferred_element_type=F32)
    o_ref[...] = (jax.nn.silu(g) * u).astype(o_ref.dtype)


def swiglu_up(h, w_gate_up, tm=2048, tn=256):
    T, K = h.shape
    F = w_gate_up.shape[1] // 2
    u_blk = F // tn
    return pl.pallas_call(
        _swiglu_up_kernel,
        grid=(T // tm, F // tn),
        in_specs=[pl.BlockSpec((tm, K), lambda i, j: (i, 0)),
                  pl.BlockSpec((K, tn), lambda i, j: (0, j)),
                  pl.BlockSpec((K, tn), lambda i, j: (0, u_blk + j))],
        out_specs=pl.BlockSpec((tm, tn), lambda i, j: (i, j)),
        out_shape=jax.ShapeDtypeStruct((T, F), BF16),
        compiler_params=_params("parallel", "arbitrary"),
        name="swiglu_up",
    )(h, w_gate_up, w_gate_up)


def _down_kernel(a_ref, w_ref, r_ref, o_ref):
    part = jnp.dot(a_ref[...], w_ref[...], preferred_element_type=F32)

    @pl.when(pl.program_id(2) == 0)
    def _():
        o_ref[...] = r_ref[...] + part

    @pl.when(pl.program_id(2) != 0)
    def _():
        o_ref[...] += part


def swiglu_down(act, w, res, tm=1024, tn=512, k_splits=2):
    T, K = act.shape
    N = w.shape[1]
    tk = K // k_splits
    return pl.pallas_call(
        _down_kernel,
        grid=(T // tm, N // tn, k_splits),
        in_specs=[pl.BlockSpec((tm, tk), lambda i, j, k: (i, k)),
                  pl.BlockSpec((tk, tn), lambda i, j, k: (k, j)),
                  pl.BlockSpec((tm, tn), lambda i, j, k: (i, j))],
        out_specs=pl.BlockSpec((tm, tn), lambda i, j, k: (i, j)),
        out_shape=jax.ShapeDtypeStruct((T, N), F32),
        compiler_params=_params("parallel", "arbitrary", "arbitrary"),
        name="swiglu_down",
    )(act, w, res)


def _split2_bf16(x):
    hi = x.astype(BF16)
    return hi, (x - hi.astype(F32)).astype(BF16)


def _forget_gate(z, lb):
    e = jnp.exp(-jnp.abs(z))
    r = 1.0 / (1.0 + e)
    pos = z >= 0
    sig = jnp.where(pos, r, e * r)
    sig_neg = jnp.where(pos, e * r, r)
    f = lb + (1.0 - lb) * sig
    return f, jnp.log(f), (1.0 - lb) * sig_neg


def _hgrn_scan_tile(q_ref, z_ref, v_ref, lb_ref, st_ref, o_scr, vt_scr, reverse):
    C = HG_CHUNK
    n_chunks = HG_TILE // C
    chunks = [slice(j * C, (j + 1) * C) for j in range(n_chunks)]
    row = lax.broadcasted_iota(jnp.int32, (HG_TILE, HG_TILE), 0)
    col = lax.broadcasted_iota(jnp.int32, (HG_TILE, HG_TILE), 1)
    visible = (col >= row) if reverse else (col <= row)
    chunk_bits = C.bit_length() - 1
    assert C == 1 << chunk_bits
    same_chunk = lax.shift_right_logical(row, chunk_bits) == lax.shift_right_logical(col, chunk_bits)
    keep = visible & same_chunk
    tri = jnp.where(keep, 1.0, 0.0).astype(BF16)
    end = 0 if reverse else C - 1
    mid = C // 2 if reverse else C // 2 - 1
    heads = [slice(g * HEAD, (g + 1) * HEAD) for g in range(HG_GROUP)]
    nt_dims = (((1,), (1,)), ((), ()))

    def rel(c, off):
        return jnp.concatenate([c[sl] - c[sl.start + off:sl.start + off + 1] for sl in chunks],
                               axis=0)

    gates, parts = [], []
    for hs in heads:
        f, logf, kk = _forget_gate(z_ref[:, hs], lb_ref[:, hs])
        gates.append((f, kk))
        parts.extend(_split2_bf16(logf))
    cs = jnp.dot(tri, jnp.concatenate(parts, axis=1), preferred_element_type=F32)
    cums = [cs[:, (2 * g) * HEAD:(2 * g + 1) * HEAD] + cs[:, (2 * g + 1) * HEAD:(2 * g + 2) * HEAD]
            for g in range(HG_GROUP)]
    mids = [rel(c, mid) for c in cums]
    worst = functools.reduce(jnp.maximum, [jnp.max(jnp.abs(a)) for a in mids])
    safe = worst < HG_MAX_EXPONENT

    @pl.when(safe)
    def _():
        q_state, decay, update = [], {}, {}
        for g, hs in enumerate(heads):
            c, a, kk = cums[g], mids[g], gates[g][1]
            q, v = q_ref[:, hs], v_ref[:, hs]
            q_in = (q * jnp.exp(a)).astype(BF16)
            k_in = (kk * jnp.exp(-a)).astype(BF16)
            s = lax.dot_general(q_in, k_in, nt_dims, preferred_element_type=F32)
            s = jnp.where(keep, s, 0.0).astype(BF16)
            o_scr[:, hs] = jnp.dot(s, v.astype(BF16), preferred_element_type=F32)
            k_st = (kk * jnp.exp(-rel(c, end))).astype(BF16)
            q_state.append((q * jnp.exp(c)).astype(BF16))
            for j, sl in enumerate(chunks):
                update[g, j] = jnp.dot(v[sl].T.astype(BF16), k_st[sl], preferred_element_type=F32)
                decay[g, j] = jnp.exp(c[sl.start + end:sl.start + end + 1])
        order = range(n_chunks - 1, -1, -1) if reverse else range(n_chunks)
        for j in order:
            sl = chunks[j]
            for g, hs in enumerate(heads):
                st = st_ref[g]
                o_scr[sl, hs] += lax.dot_general(q_state[g][sl], st.astype(BF16), nt_dims,
                                                 preferred_element_type=F32)
                st_ref[g] = st * decay[g, j] + update[g, j]

    @pl.when(jnp.logical_not(safe))
    def _():
        lane = lax.broadcasted_iota(jnp.int32, (1, HG_TILE), 1)
        rows = lax.broadcasted_iota(jnp.int32, (HG_TILE, 1), 0)
        for g, hs in enumerate(heads):
            f, kk = gates[g]
            q = q_ref[:, hs]
            vt_scr[...] = v_ref[:, hs].T

            def body(i, carry):
                t = (HG_TILE - 1 - i) if reverse else i
                onehot = (lane == t).astype(F32)
                pick = lambda x: jnp.sum(jnp.where(rows == t, x, 0.0), axis=0, keepdims=True)
                f_t, k_t, q_t = pick(f), pick(kk), pick(q)
                v_col = jnp.sum(vt_scr[...] * onehot, axis=1, keepdims=True)
                st = st_ref[g] * f_t + v_col * k_t
                st_ref[g] = st
                o_col = jnp.sum(st * q_t, axis=1, keepdims=True)
                return carry + o_col * onehot

            ot = lax.fori_loop(0, HG_TILE, body, jnp.zeros((HEAD, HG_TILE), F32))
            o_scr[:, hs] = ot.T


def _hgrn_fwd_kernel(q_ref, z_ref, v_ref, lb_ref, o_ref, st_ref, o_scr, vt_scr):
    @pl.when(pl.program_id(1) == 0)
    def _():
        st_ref[...] = jnp.zeros_like(st_ref)

    _hgrn_scan_tile(q_ref, z_ref, v_ref, lb_ref, st_ref, o_scr, vt_scr, reverse=False)
    o_ref[...] = o_scr[...]


def _hgrn_bwd_kernel(q_ref, z_ref, v_ref, lb_ref, of_ref, gate_ref, nw_ref, o_ref,
                     st_ref, o_scr, vt_scr):
    @pl.when(pl.program_id(1) == 0)
    def _():
        st_ref[...] = jnp.zeros_like(st_ref)

    _hgrn_scan_tile(q_ref, z_ref, v_ref, lb_ref, st_ref, o_scr, vt_scr, reverse=True)
    for g in range(HG_GROUP):
        hs = slice(g * HEAD, (g + 1) * HEAD)
        o = of_ref[:, hs] + o_scr[:, hs]
        o = o * lax.rsqrt(jnp.mean(o * o, axis=-1, keepdims=True) + EPS) * nw_ref[:, hs]
        o_ref[:, hs] = (o * jax.nn.silu(gate_ref[:, hs])).astype(o_ref.dtype)


def hgrn2_bidirectional(proj, lb_f, lb_b, norm_w, cols):
    T = proj.shape[0]
    n_tiles = T // HG_TILE
    W = HG_HEADS * HEAD
    GW = HG_GROUP * HEAD
    n_groups = HG_HEADS // HG_GROUP
    cq, czf, czb, ci, cg = (c // GW for c in cols)
    scratch = [pltpu.VMEM((HG_GROUP, HEAD, HEAD), F32),
               pltpu.VMEM((HG_TILE, GW), F32),
               pltpu.VMEM((HEAD, HG_TILE), F32)]
    tile = lambda blk: pl.BlockSpec((HG_TILE, GW), lambda h, i: (i, blk + h))
    vec = pl.BlockSpec((1, GW), lambda h, i: (0, h))
    o_fwd = pl.pallas_call(
        _hgrn_fwd_kernel,
        grid=(n_groups, n_tiles),
        in_specs=[tile(cq), tile(czf), tile(ci), vec],
        out_specs=pl.BlockSpec((HG_TILE, GW), lambda h, i: (i, h)),
        out_shape=jax.ShapeDtypeStruct((T, W), F32),
        scratch_shapes=scratch,
        compiler_params=_params("parallel", "arbitrary"),
        name="hgrn_fwd",
    )(proj, proj, proj, lb_f.reshape(1, W))

    last = n_tiles - 1
    rtile = lambda blk: pl.BlockSpec((HG_TILE, GW), lambda h, i: (last - i, blk + h))
    return pl.pallas_call(
        _hgrn_bwd_kernel,
        grid=(n_groups, n_tiles),
        in_specs=[rtile(cq), rtile(czb), rtile(ci), vec, rtile(0), rtile(cg), vec],
        out_specs=pl.BlockSpec((HG_TILE, GW), lambda h, i: (last - i, h)),
        out_shape=jax.ShapeDtypeStruct((T, W), BF16),
        scratch_shapes=scratch,
        compiler_params=_params("parallel", "arbitrary"),
        name="hgrn_bwd",
    )(proj, proj, proj, lb_b.reshape(1, W), o_fwd, proj, norm_w.reshape(1, W))


def _rope_tables(T):
    rows = T // GRID_W
    row = jnp.repeat(jnp.arange(rows, dtype=F32), GRID_W)
    col = jnp.tile(jnp.arange(GRID_W, dtype=F32), rows)
    axis_dim = HEAD // 2
    inv = ROPE_THETA ** (-jnp.arange(0, axis_dim, 2, dtype=F32) / axis_dim)
    ang = jnp.concatenate([row[:, None] * inv, col[:, None] * inv], axis=-1)
    cos = jnp.repeat(jnp.cos(ang), 2, axis=-1)
    sin = jnp.repeat(jnp.sin(ang), 2, axis=-1)
    sign = jnp.tile(jnp.array([-1.0, 1.0], F32), HEAD // 2)
    return cos, sin * sign


def _norm_rope_head(x, w, cos, sin_signed, scale):
    x = x * lax.rsqrt(jnp.mean(x * x, axis=-1, keepdims=True) + EPS) * w
    lane = lax.broadcasted_iota(jnp.int32, x.shape, 1)
    partner = jnp.where(lane % 2 == 0,
                        pltpu.roll(x, HEAD - 1, axis=1),
                        pltpu.roll(x, 1, axis=1))
    out = x * cos + partner * sin_signed
    return out * scale if scale is not None else out


def _qk_rope_kernel(q_ref, k_ref, v_ref, qw_ref, kw_ref, cos_ref, sin_ref,
                    qt_ref, ko_ref, vt_ref, stat_ref, *, q_scale):
    cos, sin = cos_ref[...], sin_ref[...]

    def max_sq_norm(x_bf16):
        x = x_bf16.astype(F32)
        return jnp.max(jnp.sum(x * x, axis=-1, keepdims=True))

    q_sq = k_sq = None
    for h in range(ATT_Q_HEADS):
        sl = slice(h * HEAD, (h + 1) * HEAD)
        q = _norm_rope_head(q_ref[:, sl], qw_ref[...], cos, sin, q_scale).astype(qt_ref.dtype)
        qt_ref[sl, :] = q.astype(F32).T.astype(qt_ref.dtype)
        n = max_sq_norm(q)
        q_sq = n if q_sq is None else jnp.maximum(q_sq, n)
    for h in range(ATT_KV_HEADS):
        sl = slice(h * HEAD, (h + 1) * HEAD)
        k = _norm_rope_head(k_ref[:, sl], kw_ref[...], cos, sin, None).astype(ko_ref.dtype)
        ko_ref[:, sl] = k
        n = max_sq_norm(k)
        k_sq = n if k_sq is None else jnp.maximum(k_sq, n)
        vt_ref[0, sl, :] = v_ref[:, sl].T.astype(vt_ref.dtype)
    stat_ref[0, 0:1, :] = jnp.full((1, HEAD), q_sq, F32)
    stat_ref[0, 1:2, :] = jnp.full((1, HEAD), k_sq, F32)


def qk_norm_rope(proj, q_w, k_w, cols):
    T = proj.shape[0]
    tm = ATT_SUB_K
    QW, KW = ATT_Q_HEADS * HEAD, ATT_KV_HEADS * HEAD
    cq, ck, cv = cols
    cos, sin = _rope_tables(T)
    q_scale = float(np.log2(np.e) / np.sqrt(np.float32(HEAD)))
    qt, k, vt, stats = pl.pallas_call(
        functools.partial(_qk_rope_kernel, q_scale=q_scale),
        grid=(T // tm,),
        in_specs=[pl.BlockSpec((tm, QW), lambda i: (i, cq // QW)),
                  pl.BlockSpec((tm, KW), lambda i: (i, ck // KW)),
                  pl.BlockSpec((tm, KW), lambda i: (i, cv // KW)),
                  pl.BlockSpec((1, HEAD), lambda i: (0, 0)),
                  pl.BlockSpec((1, HEAD), lambda i: (0, 0)),
                  pl.BlockSpec((tm, HEAD), lambda i: (i, 0)),
                  pl.BlockSpec((tm, HEAD), lambda i: (i, 0))],
        out_specs=[pl.BlockSpec((QW, tm), lambda i: (0, i)),
                   pl.BlockSpec((tm, KW), lambda i: (i, 0)),
                   pl.BlockSpec((1, KW, tm), lambda i: (i, 0, 0)),
                   pl.BlockSpec((1, 2, HEAD), lambda i: (i, 0, 0))],
        out_shape=[jax.ShapeDtypeStruct((QW, T), BF16),
                   jax.ShapeDtypeStruct((T, KW), BF16),
                   jax.ShapeDtypeStruct((T // tm, KW, tm), BF16),
                   jax.ShapeDtypeStruct((T // tm, 2, HEAD), F32)],
        compiler_params=_params("parallel"),
        name="qk_norm_rope",
    )(proj, proj, proj, q_w.reshape(1, HEAD), k_w.reshape(1, HEAD), cos, sin)
    q_max = jnp.sqrt(jnp.max(stats[:, 0, 0]))
    k_max = jnp.sqrt(jnp.max(stats[:, 1, 0]))
    return qt, k, vt, q_max, k_max


def _flash_kernel(bounded_ref, qt_ref, k_ref, vt_ref, kmax_ref, o_ref, m_scr, l_scr, acc_scr):
    j = pl.program_id(2)
    group = ATT_Q_HEADS // ATT_KV_HEADS
    n_sub = k_ref.shape[0] // ATT_SUB_K
    bounded = bounded_ref[0] != 0

    def key_block(jj):
        k = k_ref[pl.ds(pl.multiple_of(jj * ATT_SUB_K, ATT_SUB_K), ATT_SUB_K), :]
        return k, vt_ref[jj]

    def logits(k, r):
        return jnp.dot(k, qt_ref[r * HEAD:(r + 1) * HEAD, :], preferred_element_type=F32)

    @pl.when(j == 0)
    def _():
        l_scr[...] = jnp.zeros_like(l_scr)
        acc_scr[...] = jnp.zeros_like(acc_scr)

        @pl.when(bounded)
        def _():
            for r in range(group):
                q = qt_ref[r * HEAD:(r + 1) * HEAD, :].astype(F32)
                m_scr[r] = jnp.sqrt(jnp.sum(q * q, axis=0, keepdims=True)) * kmax_ref[...]

        @pl.when(jnp.logical_not(bounded))
        def _():
            m_scr[...] = jnp.full_like(m_scr, -jnp.inf)

    @pl.when(bounded)
    def _():
        stages = [(jj, r) for jj in range(n_sub) for r in range(group)]

        def stage_logits(stage):
            jj, r = stage
            return logits(k_ref[jj * ATT_SUB_K:(jj + 1) * ATT_SUB_K, :], r)

        st = stage_logits(stages[0])
        for idx, (jj, r) in enumerate(stages):
            st_next = stage_logits(stages[idx + 1]) if idx + 1 < len(stages) else None
            pt = jnp.exp2(st - m_scr[r])
            l_scr[r] += jnp.sum(pt, axis=0, keepdims=True)
            acc_scr[r] += jnp.dot(vt_ref[jj], pt.astype(BF16), preferred_element_type=F32)
            st = st_next

    @pl.when(jnp.logical_not(bounded))
    def _():
        def sub_block(jj, carry):
            k, vt = key_block(jj)
            for r in range(group):
                st = logits(k, r)
                m_prev = m_scr[r]
                m_new = jnp.maximum(m_prev, jnp.max(st, axis=0, keepdims=True))
                alpha = jnp.exp2(m_prev - m_new)
                pt = jnp.exp2(st - m_new)
                l_scr[r] = alpha * l_scr[r] + jnp.sum(pt, axis=0, keepdims=True)
                acc_scr[r] = alpha * acc_scr[r] + jnp.dot(vt, pt.astype(BF16),
                                                          preferred_element_type=F32)
                m_scr[r] = m_new
            return carry

        lax.fori_loop(0, n_sub, sub_block, 0)

    @pl.when(j == pl.num_programs(2) - 1)
    def _():
        for r in range(group):
            o = acc_scr[r] / l_scr[r]
            o_ref[:, r * HEAD:(r + 1) * HEAD] = o.T.astype(o_ref.dtype)


def flash_attention(qt, k, vt, q_max, k_max, tq=512, tk=4096):
    T = k.shape[0]
    group = ATT_Q_HEADS // ATT_KV_HEADS
    GW = group * HEAD
    bounded = (q_max * k_max <= ATT_MAX_BOUNDED_LOGIT).astype(jnp.int32).reshape(1)
    kmax_row = jnp.full((1, tq), k_max, F32)
    grid_spec = pltpu.PrefetchScalarGridSpec(
        num_scalar_prefetch=1,
        grid=(ATT_KV_HEADS, T // tq, T // tk),
        in_specs=[pl.BlockSpec((GW, tq), lambda g, i, j, b: (g, i)),
                  pl.BlockSpec((tk, HEAD), lambda g, i, j, b: (j, g)),
                  pl.BlockSpec((tk // ATT_SUB_K, HEAD, ATT_SUB_K), lambda g, i, j, b: (j, g, 0)),
                  pl.BlockSpec((1, tq), lambda g, i, j, b: (0, 0))],
        out_specs=pl.BlockSpec((tq, GW), lambda g, i, j, b: (i, g)),
        scratch_shapes=[pltpu.VMEM((group, 1, tq), F32),
                        pltpu.VMEM((group, 1, tq), F32),
                        pltpu.VMEM((group, HEAD, tq), F32)])
    return pl.pallas_call(
        _flash_kernel,
        grid_spec=grid_spec,
        out_shape=jax.ShapeDtypeStruct((T, ATT_Q_HEADS * HEAD), BF16),
        compiler_params=_params("parallel", "parallel", "arbitrary"),
        name="flash_attention",
    )(bounded, qt, k, vt, kmax_row)


def _encoder_layer(x, norm_mix_w, w_in, lb_f, lb_b, hg_norm_w, q_norm_w, k_norm_w,
                   w_branch_a, w_branch_b, w_out, norm_ffn_w, w_gate_up, w_down):
    hg_w = HG_HEADS * HEAD
    q_w, kv_w = ATT_Q_HEADS * HEAD, ATT_KV_HEADS * HEAD
    D = x.shape[1]
    sizes = (hg_w, hg_w, hg_w, hg_w, hg_w, q_w, kv_w, kv_w, D, D)
    offs = np.concatenate([[0], np.cumsum(sizes)]).tolist()
    assert offs[-1] == w_in.shape[1]

    h = rmsnorm(x, norm_mix_w, BF16)
    proj = in_projection(h, w_in)
    o_hg = hgrn2_bidirectional(proj, lb_f, lb_b, hg_norm_w, offs[0:5])
    o_att = flash_attention(*qk_norm_rope(proj, q_norm_w, k_norm_w, offs[5:8]))
    merged = gated_merge(o_hg, o_att, w_branch_a, w_branch_b, proj, offs[8], offs[9])
    x = residual_matmul(merged, w_out, x)
    h = rmsnorm(x, norm_ffn_w, BF16)
    act = swiglu_up(h, w_gate_up)
    return swiglu_down(act, w_down, x)


def kernel(x_prompt, x_sample, norm_mix_w, w_in, lb_fwd, lb_bwd, hg_norm_w, q_norm_w, k_norm_w,
           w_branch_a, w_branch_b, w_out, norm_ffn_w, w_gate_up, w_down, norm_final_w):
    depth = w_in.shape[0]
    lbs_f = jnp.cumsum(jax.nn.softmax(lb_fwd.astype(F32), axis=0), axis=0)
    lbs_b = jnp.cumsum(jax.nn.softmax(lb_bwd.astype(F32), axis=0), axis=0)
    bf = lambda w: w.astype(BF16)
    w_in, w_branch_a, w_branch_b, w_out, w_gate_up, w_down = map(
        bf, (w_in, w_branch_a, w_branch_b, w_out, w_gate_up, w_down))

    def trunk(x):
        x = x.reshape(x.shape[-2], x.shape[-1])
        for l in range(depth):
            x = _encoder_layer(x, norm_mix_w[l], w_in[l], lbs_f[l], lbs_b[l], hg_norm_w[l],
                               q_norm_w[l], k_norm_w[l], w_branch_a[l], w_branch_b[l], w_out[l],
                               norm_ffn_w[l], w_gate_up[l], w_down[l])
        return rmsnorm(x, norm_final_w, F32)

    assert x_prompt.shape[0] == 1 and x_sample.shape[0] == 1
    return trunk(x_prompt)[None], trunk(x_sample)[None]
```

```python
import functools

import jax
import jax.numpy as jnp
import numpy as np
from jax import lax
from jax.experimental import pallas as pl
from jax.experimental.pallas import tpu as pltpu

F32 = jnp.float32
BF16 = jnp.bfloat16

EPS = 1e-6
HEAD = 128
HG_HEADS = 16
ATT_Q_HEADS = 16
ATT_KV_HEADS = 4
GRID_W = 64
ROPE_THETA = 10000.0

V7X_VMEM_LIMIT_BYTES = 56 * 1024 * 1024

HG_CHUNK = 64
HG_TILE = 256
HG_GROUP = 4
HG_MAX_EXPONENT = 80.0

ATT_SUB_K = 512
ATT_MAX_BOUNDED_LOGIT = 60.0


def _params(*semantics):
    return pltpu.CompilerParams(dimension_semantics=semantics,
                                vmem_limit_bytes=V7X_VMEM_LIMIT_BYTES)


def _sigmoid(x):
    return 0.5 + 0.5 * jnp.tanh(0.5 * x)


def _silu(x):
    return x * _sigmoid(x)


def _rmsnorm_kernel(x_ref, w_ref, o_ref):
    x = x_ref[...]
    inv = lax.rsqrt(jnp.mean(x * x, axis=-1, keepdims=True) + EPS)
    o_ref[...] = (x * inv * w_ref[...]).astype(o_ref.dtype)


def rmsnorm(x, w, out_dtype, tm=256):
    T, D = x.shape
    return pl.pallas_call(
        _rmsnorm_kernel,
        grid=(T // tm,),
        in_specs=[pl.BlockSpec((tm, D), lambda i: (i, 0)),
                  pl.BlockSpec((1, D), lambda i: (0, 0))],
        out_specs=pl.BlockSpec((tm, D), lambda i: (i, 0)),
        out_shape=jax.ShapeDtypeStruct((T, D), out_dtype),
        compiler_params=_params("parallel"),
        name="rmsnorm",
    )(x, w.reshape(1, D))


def _proj_kernel(a_ref, w_ref, o_ref):
    o_ref[...] = jnp.dot(a_ref[...], w_ref[...], preferred_element_type=F32)


def in_projection(h, w, tm=1024, tn=1024):
    T, K = h.shape
    N = w.shape[1]
    return pl.pallas_call(
        _proj_kernel,
        grid=(T // tm, N // tn),
        in_specs=[pl.BlockSpec((tm, K), lambda i, j: (i, 0)),
                  pl.BlockSpec((K, tn), lambda i, j: (0, j))],
        out_specs=pl.BlockSpec((tm, tn), lambda i, j: (i, j)),
        out_shape=jax.ShapeDtypeStruct((T, N), F32),
        compiler_params=_params("parallel", "arbitrary"),
        name="in_projection",
    )(h, w)


def _merge_kernel(oa_ref, ob_ref, wa_ref, wb_ref, ga_ref, gb_ref, o_ref):
    ya = jnp.dot(oa_ref[...], wa_ref[...], preferred_element_type=F32)
    yb = jnp.dot(ob_ref[...], wb_ref[...], preferred_element_type=F32)
    o_ref[...] = (_sigmoid(ga_ref[...]) * ya + _sigmoid(gb_ref[...]) * yb).astype(o_ref.dtype)


def gated_merge(o_hg, o_att, w_a, w_b, proj, ga_col, gb_col, tm=1024, tn=512):
    T, K = o_hg.shape
    D = w_a.shape[1]
    ga_blk, gb_blk = ga_col // tn, gb_col // tn
    return pl.pallas_call(
        _merge_kernel,
        grid=(T // tm, D // tn),
        in_specs=[pl.BlockSpec((tm, K), lambda i, j: (i, 0)),
                  pl.BlockSpec((tm, K), lambda i, j: (i, 0)),
                  pl.BlockSpec((K, tn), lambda i, j: (0, j)),
                  pl.BlockSpec((K, tn), lambda i, j: (0, j)),
                  pl.BlockSpec((tm, tn), lambda i, j: (i, ga_blk + j)),
                  pl.BlockSpec((tm, tn), lambda i, j: (i, gb_blk + j))],
        out_specs=pl.BlockSpec((tm, tn), lambda i, j: (i, j)),
        out_shape=jax.ShapeDtypeStruct((T, D), BF16),
        compiler_params=_params("parallel", "arbitrary"),
        name="gated_merge",
    )(o_hg, o_att, w_a, w_b, proj, proj)


def _residual_mm_kernel(a_ref, w_ref, r_ref, o_ref):
    o_ref[...] = r_ref[...] + jnp.dot(a_ref[...], w_ref[...], preferred_element_type=F32)


def residual_matmul(a, w, res, tm=1024, tn=1024):
    T, K = a.shape
    N = w.shape[1]
    return pl.pallas_call(
        _residual_mm_kernel,
        grid=(T // tm, N // tn),
        in_specs=[pl.BlockSpec((tm, K), lambda i, j: (i, 0)),
                  pl.BlockSpec((K, tn), lambda i, j: (0, j)),
                  pl.BlockSpec((tm, tn), lambda i, j: (i, j))],
        out_specs=pl.BlockSpec((tm, tn), lambda i, j: (i, j)),
        out_shape=jax.ShapeDtypeStruct((T, N), F32),
        compiler_params=_params("parallel", "arbitrary"),
        name="residual_matmul",
    )(a, w, res)


def _swiglu_up_kernel(a_ref, wg_ref, wu_ref, o_ref):
    a = a_ref[...]
    g = jnp.dot(a, wg_ref[...], preferred_element_type=F32)
    u = jnp.dot(a, wu_ref[...], preferred_element_type=F32)
    o_ref[...] = (_silu(g) * u).astype(o_ref.dtype)


def swiglu_up(h, w_gate_up, tm=2048, tn=256):
    T, K = h.shape
    F = w_gate_up.shape[1] // 2
    u_blk = F // tn
    return pl.pallas_call(
        _swiglu_up_kernel,
        grid=(T // tm, F // tn),
        in_specs=[pl.BlockSpec((tm, K), lambda i, j: (i, 0)),
                  pl.BlockSpec((K, tn), lambda i, j: (0, j)),
                  pl.BlockSpec((K, tn), lambda i, j: (0, u_blk + j))],
        out_specs=pl.BlockSpec((tm, tn), lambda i, j: (i, j)),
        out_shape=jax.ShapeDtypeStruct((T, F), BF16),
        compiler_params=_params("parallel", "arbitrary"),
        name="swiglu_up",
    )(h, w_gate_up, w_gate_up)


def _down_kernel(a_ref, w_ref, r_ref, o_ref):
    part = jnp.dot(a_ref[...], w_ref[...], preferred_element_type=F32)

    @pl.when(pl.program_id(2) == 0)
    def _():
        o_ref[...] = r_ref[...] + part

    @pl.when(pl.program_id(2) != 0)
    def _():
        o_ref[...] += part


def swiglu_down(act, w, res, tm=512, tn=512, k_splits=1):
    T, K = act.shape
    N = w.shape[1]
    tk = K // k_splits
    return pl.pallas_call(
        _down_kernel,
        grid=(T // tm, N // tn, k_splits),
        in_specs=[pl.BlockSpec((tm, tk), lambda i, j, k: (i, k)),
                  pl.BlockSpec((tk, tn), lambda i, j, k: (k, j)),
                  pl.BlockSpec((tm, tn), lambda i, j, k: (i, j))],
        out_specs=pl.BlockSpec((tm, tn), lambda i, j, k: (i, j)),
        out_shape=jax.ShapeDtypeStruct((T, N), F32),
        compiler_params=_params("parallel", "arbitrary", "arbitrary"),
        name="swiglu_down",
    )(act, w, res)


def _split2_bf16(x):
    hi = x.astype(BF16)
    return hi, (x - hi.astype(F32)).astype(BF16)


def _forget_gate(z, lb):
    half = 0.5 * (1.0 - lb)
    kk = half - half * jnp.tanh(0.5 * z)
    f = 1.0 - kk
    return f, jnp.log(f), kk


def _hgrn_scan_tile(q_ref, z_ref, v_ref, lb_ref, st_ref, st_save, o_scr, vt_scr, emit, reverse):
    C = HG_CHUNK
    n_chunks = HG_TILE // C
    chunks = [slice(j * C, (j + 1) * C) for j in range(n_chunks)]
    row = lax.broadcasted_iota(jnp.int32, (HG_TILE, HG_TILE), 0)
    col = lax.broadcasted_iota(jnp.int32, (HG_TILE, HG_TILE), 1)
    visible = (col >= row) if reverse else (col <= row)
    chunk_bits = C.bit_length() - 1
    assert C == 1 << chunk_bits
    same_chunk = lax.shift_right_logical(row, chunk_bits) == lax.shift_right_logical(col, chunk_bits)
    keep = visible & same_chunk
    tri = jnp.where(keep, 1.0, 0.0).astype(BF16)
    end = 0 if reverse else C - 1
    mid = C // 2 if reverse else C // 2 - 1
    heads = [slice(g * HEAD, (g + 1) * HEAD) for g in range(HG_GROUP)]
    nt_dims = (((1,), (1,)), ((), ()))

    def rel(c, off):
        return jnp.concatenate([c[sl] - c[sl.start + off:sl.start + off + 1] for sl in chunks],
                               axis=0)

    st_save[...] = st_ref[...]

    gates, parts = [], []
    for hs in heads:
        f, logf, kk = _forget_gate(z_ref[:, hs], lb_ref[:, hs])
        gates.append((f, kk))
        parts.extend(_split2_bf16(logf))
    cs = jnp.dot(tri, jnp.concatenate(parts, axis=1), preferred_element_type=F32)
    cums = [cs[:, (2 * g) * HEAD:(2 * g + 1) * HEAD] + cs[:, (2 * g + 1) * HEAD:(2 * g + 2) * HEAD]
            for g in range(HG_GROUP)]
    mids = [rel(c, mid) for c in cums]
    worst = functools.reduce(jnp.maximum, [jnp.max(jnp.abs(a)) for a in mids])

    q_state, decay, update = [], {}, {}
    for g, hs in enumerate(heads):
        c, a, kk = cums[g], mids[g], gates[g][1]
        q, v = q_ref[:, hs], v_ref[:, hs]
        q_in = (q * jnp.exp(a)).astype(BF16)
        k_in = (kk * jnp.exp(-a)).astype(BF16)
        s = lax.dot_general(q_in, k_in, nt_dims, preferred_element_type=F32)
        s = jnp.where(keep, s, 0.0).astype(BF16)
        o_scr[:, hs] = jnp.dot(s, v.astype(BF16), preferred_element_type=F32)
        k_st = (kk * jnp.exp(-rel(c, end))).astype(BF16)
        q_state.append((q * jnp.exp(c)).astype(BF16))
        for j, sl in enumerate(chunks):
            update[g, j] = jnp.dot(v[sl].T.astype(BF16), k_st[sl], preferred_element_type=F32)
            decay[g, j] = jnp.exp(c[sl.start + end:sl.start + end + 1])
    order = range(n_chunks - 1, -1, -1) if reverse else range(n_chunks)
    for j in order:
        sl = chunks[j]
        for g, hs in enumerate(heads):
            st = st_ref[g]
            o_scr[sl, hs] += lax.dot_general(q_state[g][sl], st.astype(BF16), nt_dims,
                                             preferred_element_type=F32)
            st_ref[g] = st * decay[g, j] + update[g, j]
    emit()

    @pl.when(jnp.logical_not(worst < HG_MAX_EXPONENT))
    def _():
        st_ref[...] = st_save[...]
        lane = lax.broadcasted_iota(jnp.int32, (1, HG_TILE), 1)
        rows = lax.broadcasted_iota(jnp.int32, (HG_TILE, 1), 0)
        for g, hs in enumerate(heads):
            z, lb = z_ref[:, hs], lb_ref[:, hs]
            q = q_ref[:, hs]
            vt_scr[...] = v_ref[:, hs].T

            def body(i, carry):
                t = (HG_TILE - 1 - i) if reverse else i
                onehot = (lane == t).astype(F32)
                pick = lambda x: jnp.sum(jnp.where(rows == t, x, 0.0), axis=0, keepdims=True)
                f_t, _, k_t = _forget_gate(pick(z), lb)
                q_t = pick(q)
                v_col = jnp.sum(vt_scr[...] * onehot, axis=1, keepdims=True)
                st = st_ref[g] * f_t + v_col * k_t
                st_ref[g] = st
                o_col = jnp.sum(st * q_t, axis=1, keepdims=True)
                return carry + o_col * onehot

            ot = lax.fori_loop(0, HG_TILE, body, jnp.zeros((HEAD, HG_TILE), F32))
            o_scr[:, hs] = ot.T
        emit()


def _hgrn_fwd_kernel(q_ref, z_ref, v_ref, lb_ref, o_ref, st_ref, st_save, o_scr, vt_scr):
    @pl.when(pl.program_id(1) == 0)
    def _():
        st_ref[...] = jnp.zeros_like(st_ref)

    def emit():
        o_ref[...] = o_scr[...]

    _hgrn_scan_tile(q_ref, z_ref, v_ref, lb_ref, st_ref, st_save, o_scr, vt_scr, emit,
                    reverse=False)


def _hgrn_bwd_kernel(q_ref, z_ref, v_ref, lb_ref, of_ref, gate_ref, nw_ref, o_ref,
                     st_ref, st_save, o_scr, vt_scr):
    @pl.when(pl.program_id(1) == 0)
    def _():
        st_ref[...] = jnp.zeros_like(st_ref)

    def emit():
        for g in range(HG_GROUP):
            hs = slice(g * HEAD, (g + 1) * HEAD)
            o = of_ref[:, hs] + o_scr[:, hs]
            o = o * lax.rsqrt(jnp.mean(o * o, axis=-1, keepdims=True) + EPS) * nw_ref[:, hs]
            o_ref[:, hs] = (o * _silu(gate_ref[:, hs])).astype(o_ref.dtype)

    _hgrn_scan_tile(q_ref, z_ref, v_ref, lb_ref, st_ref, st_save, o_scr, vt_scr, emit,
                    reverse=True)


def hgrn2_bidirectional(proj, lb_f, lb_b, norm_w, cols):
    T = proj.shape[0]
    n_tiles = T // HG_TILE
    W = HG_HEADS * HEAD
    GW = HG_GROUP * HEAD
    n_groups = HG_HEADS // HG_GROUP
    cq, czf, czb, ci, cg = (c // GW for c in cols)
    scratch = [pltpu.VMEM((HG_GROUP, HEAD, HEAD), F32),
               pltpu.VMEM((HG_GROUP, HEAD, HEAD), F32),
               pltpu.VMEM((HG_TILE, GW), F32),
               pltpu.VMEM((HEAD, HG_TILE), F32)]
    tile = lambda blk: pl.BlockSpec((HG_TILE, GW), lambda h, i: (i, blk + h))
    vec = pl.BlockSpec((1, GW), lambda h, i: (0, h))
    o_fwd = pl.pallas_call(
        _hgrn_fwd_kernel,
        grid=(n_groups, n_tiles),
        in_specs=[tile(cq), tile(czf), tile(ci), vec],
        out_specs=pl.BlockSpec((HG_TILE, GW), lambda h, i: (i, h)),
        out_shape=jax.ShapeDtypeStruct((T, W), F32),
        scratch_shapes=scratch,
        compiler_params=_params("parallel", "arbitrary"),
        name="hgrn_fwd",
    )(proj, proj, proj, lb_f.reshape(1, W))

    last = n_tiles - 1
    rtile = lambda blk: pl.BlockSpec((HG_TILE, GW), lambda h, i: (last - i, blk + h))
    return pl.pallas_call(
        _hgrn_bwd_kernel,
        grid=(n_groups, n_tiles),
        in_specs=[rtile(cq), rtile(czb), rtile(ci), vec, rtile(0), rtile(cg), vec],
        out_specs=pl.BlockSpec((HG_TILE, GW), lambda h, i: (last - i, h)),
        out_shape=jax.ShapeDtypeStruct((T, W), BF16),
        scratch_shapes=scratch,
        compiler_params=_params("parallel", "arbitrary"),
        name="hgrn_bwd",
    )(proj, proj, proj, lb_b.reshape(1, W), o_fwd, proj, norm_w.reshape(1, W))


def _rope_tables(T):
    rows = T // GRID_W
    row = jnp.repeat(jnp.arange(rows, dtype=F32), GRID_W)
    col = jnp.tile(jnp.arange(GRID_W, dtype=F32), rows)
    axis_dim = HEAD // 2
    inv = ROPE_THETA ** (-jnp.arange(0, axis_dim, 2, dtype=F32) / axis_dim)
    ang = jnp.concatenate([row[:, None] * inv, col[:, None] * inv], axis=-1)
    cos = jnp.repeat(jnp.cos(ang), 2, axis=-1)
    sin = jnp.repeat(jnp.sin(ang), 2, axis=-1)
    sign = jnp.tile(jnp.array([-1.0, 1.0], F32), HEAD // 2)
    return cos, sin * sign


def _norm_rope_head(x, w, cos, sin_signed, scale):
    x = x * lax.rsqrt(jnp.mean(x * x, axis=-1, keepdims=True) + EPS) * w
    lane = lax.broadcasted_iota(jnp.int32, x.shape, 1)
    partner = jnp.where(lane % 2 == 0,
                        pltpu.roll(x, HEAD - 1, axis=1),
                        pltpu.roll(x, 1, axis=1))
    out = x * cos + partner * sin_signed
    return out * scale if scale is not None else out


def _qk_rope_kernel(q_ref, k_ref, v_ref, qw_ref, kw_ref, cos_ref, sin_ref,
                    qt_ref, ko_ref, vt_ref, stat_ref, *, q_scale):
    cos, sin = cos_ref[...], sin_ref[...]

    def max_sq_norm(x_bf16):
        x = x_bf16.astype(F32)
        return jnp.max(jnp.sum(x * x, axis=-1, keepdims=True))

    q_sq = k_sq = None
    for h in range(ATT_Q_HEADS):
        sl = slice(h * HEAD, (h + 1) * HEAD)
        q = _norm_rope_head(q_ref[:, sl], qw_ref[...], cos, sin, q_scale).astype(qt_ref.dtype)
        qt_ref[sl, :] = q.astype(F32).T.astype(qt_ref.dtype)
        n = max_sq_norm(q)
        q_sq = n if q_sq is None else jnp.maximum(q_sq, n)
    for h in range(ATT_KV_HEADS):
        sl = slice(h * HEAD, (h + 1) * HEAD)
        k = _norm_rope_head(k_ref[:, sl], kw_ref[...], cos, sin, None).astype(ko_ref.dtype)
        ko_ref[:, sl] = k
        n = max_sq_norm(k)
        k_sq = n if k_sq is None else jnp.maximum(k_sq, n)
        vt_ref[0, sl, :] = v_ref[:, sl].T.astype(vt_ref.dtype)
    stat_ref[0, 0:1, :] = jnp.full((1, HEAD), q_sq, F32)
    stat_ref[0, 1:2, :] = jnp.full((1, HEAD), k_sq, F32)


def qk_norm_rope(proj, q_w, k_w, cols):
    T = proj.shape[0]
    tm = ATT_SUB_K
    QW, KW = ATT_Q_HEADS * HEAD, ATT_KV_HEADS * HEAD
    cq, ck, cv = cols
    cos, sin = _rope_tables(T)
    q_scale = float(np.log2(np.e) / np.sqrt(np.float32(HEAD)))
    qt, k, vt, stats = pl.pallas_call(
        functools.partial(_qk_rope_kernel, q_scale=q_scale),
        grid=(T // tm,),
        in_specs=[pl.BlockSpec((tm, QW), lambda i: (i, cq // QW)),
                  pl.BlockSpec((tm, KW), lambda i: (i, ck // KW)),
                  pl.BlockSpec((tm, KW), lambda i: (i, cv // KW)),
                  pl.BlockSpec((1, HEAD), lambda i: (0, 0)),
                  pl.BlockSpec((1, HEAD), lambda i: (0, 0)),
                  pl.BlockSpec((tm, HEAD), lambda i: (i, 0)),
                  pl.BlockSpec((tm, HEAD), lambda i: (i, 0))],
        out_specs=[pl.BlockSpec((QW, tm), lambda i: (0, i)),
                   pl.BlockSpec((tm, KW), lambda i: (i, 0)),
                   pl.BlockSpec((1, KW, tm), lambda i: (i, 0, 0)),
                   pl.BlockSpec((1, 2, HEAD), lambda i: (i, 0, 0))],
        out_shape=[jax.ShapeDtypeStruct((QW, T), BF16),
                   jax.ShapeDtypeStruct((T, KW), BF16),
                   jax.ShapeDtypeStruct((T // tm, KW, tm), BF16),
                   jax.ShapeDtypeStruct((T // tm, 2, HEAD), F32)],
        compiler_params=_params("parallel"),
        name="qk_norm_rope",
    )(proj, proj, proj, q_w.reshape(1, HEAD), k_w.reshape(1, HEAD), cos, sin)
    q_max = jnp.sqrt(jnp.max(stats[:, 0, 0]))
    k_max = jnp.sqrt(jnp.max(stats[:, 1, 0]))
    return qt, k, vt, q_max, k_max


def _flash_kernel(bounded_ref, qt_ref, k_ref, vt_ref, kmax_ref, o_ref, m_scr, l_scr, acc_scr):
    j = pl.program_id(2)
    group = ATT_Q_HEADS // ATT_KV_HEADS
    n_sub = k_ref.shape[0] // ATT_SUB_K
    bounded = bounded_ref[0] != 0

    def key_block(jj):
        k = k_ref[pl.ds(pl.multiple_of(jj * ATT_SUB_K, ATT_SUB_K), ATT_SUB_K), :]
        return k, vt_ref[jj]

    def logits(k, r):
        return jnp.dot(k, qt_ref[r * HEAD:(r + 1) * HEAD, :], preferred_element_type=F32)

    @pl.when(j == 0)
    def _():
        l_scr[...] = jnp.zeros_like(l_scr)
        acc_scr[...] = jnp.zeros_like(acc_scr)

        @pl.when(bounded)
        def _():
            for r in range(group):
                q = qt_ref[r * HEAD:(r + 1) * HEAD, :].astype(F32)
                m_scr[r] = jnp.sqrt(jnp.sum(q * q, axis=0, keepdims=True)) * kmax_ref[...]

        @pl.when(jnp.logical_not(bounded))
        def _():
            m_scr[...] = jnp.full_like(m_scr, -jnp.inf)

    @pl.when(bounded)
    def _():
        stages = [(jj, r) for jj in range(n_sub) for r in range(group)]

        def stage_logits(stage):
            jj, r = stage
            return logits(k_ref[jj * ATT_SUB_K:(jj + 1) * ATT_SUB_K, :], r)

        st = stage_logits(stages[0])
        for idx, (jj, r) in enumerate(stages):
            st_next = stage_logits(stages[idx + 1]) if idx + 1 < len(stages) else None
            pt = jnp.exp2(st - m_scr[r])
            l_scr[r] += jnp.sum(pt, axis=0, keepdims=True)
            acc_scr[r] += jnp.dot(vt_ref[jj], pt.astype(BF16), preferred_element_type=F32)
            st = st_next

    @pl.when(jnp.logical_not(bounded))
    def _():
        def sub_block(jj, carry):
            k, vt = key_block(jj)
            for r in range(group):
                st = logits(k, r)
                m_prev = m_scr[r]
                m_new = jnp.maximum(m_prev, jnp.max(st, axis=0, keepdims=True))
                alpha = jnp.exp2(m_prev - m_new)
                pt = jnp.exp2(st - m_new)
                l_scr[r] = alpha * l_scr[r] + jnp.sum(pt, axis=0, keepdims=True)
                acc_scr[r] = alpha * acc_scr[r] + jnp.dot(vt, pt.astype(BF16),
                                                          preferred_element_type=F32)
                m_scr[r] = m_new
            return carry

        lax.fori_loop(0, n_sub, sub_block, 0)

    @pl.when(j == pl.num_programs(2) - 1)
    def _():
        for r in range(group):
            o = acc_scr[r] / l_scr[r]
            o_ref[:, r * HEAD:(r + 1) * HEAD] = o.T.astype(o_ref.dtype)


def flash_attention(qt, k, vt, q_max, k_max, tq=512, tk=4096):
    T = k.shape[0]
    group = ATT_Q_HEADS // ATT_KV_HEADS
    GW = group * HEAD
    bounded = (q_max * k_max <= ATT_MAX_BOUNDED_LOGIT).astype(jnp.int32).reshape(1)
    kmax_row = jnp.full((1, tq), k_max, F32)
    grid_spec = pltpu.PrefetchScalarGridSpec(
        num_scalar_prefetch=1,
        grid=(ATT_KV_HEADS, T // tq, T // tk),
        in_specs=[pl.BlockSpec((GW, tq), lambda g, i, j, b: (g, i)),
                  pl.BlockSpec((tk, HEAD), lambda g, i, j, b: (j, g)),
                  pl.BlockSpec((tk // ATT_SUB_K, HEAD, ATT_SUB_K), lambda g, i, j, b: (j, g, 0)),
                  pl.BlockSpec((1, tq), lambda g, i, j, b: (0, 0))],
        out_specs=pl.BlockSpec((tq, GW), lambda g, i, j, b: (i, g)),
        scratch_shapes=[pltpu.VMEM((group, 1, tq), F32),
                        pltpu.VMEM((group, 1, tq), F32),
                        pltpu.VMEM((group, HEAD, tq), F32)])
    return pl.pallas_call(
        _flash_kernel,
        grid_spec=grid_spec,
        out_shape=jax.ShapeDtypeStruct((T, ATT_Q_HEADS * HEAD), BF16),
        compiler_params=_params("parallel", "parallel", "arbitrary"),
        name="flash_attention",
    )(bounded, qt, k, vt, kmax_row)


def _encoder_layer(x, norm_mix_w, w_in, lb_f, lb_b, hg_norm_w, q_norm_w, k_norm_w,
                   w_branch_a, w_branch_b, w_out, norm_ffn_w, w_gate_up, w_down):
    hg_w = HG_HEADS * HEAD
    q_w, kv_w = ATT_Q_HEADS * HEAD, ATT_KV_HEADS * HEAD
    D = x.shape[1]
    sizes = (hg_w, hg_w, hg_w, hg_w, hg_w, q_w, kv_w, kv_w, D, D)
    offs = np.concatenate([[0], np.cumsum(sizes)]).tolist()
    assert offs[-1] == w_in.shape[1]

    h = rmsnorm(x, norm_mix_w, BF16)
    proj = in_projection(h, w_in)
    o_hg = hgrn2_bidirectional(proj, lb_f, lb_b, hg_norm_w, offs[0:5])
    o_att = flash_attention(*qk_norm_rope(proj, q_norm_w, k_norm_w, offs[5:8]))
    merged = gated_merge(o_hg, o_att, w_branch_a, w_branch_b, proj, offs[8], offs[9])
    x = residual_matmul(merged, w_out, x)
    h = rmsnorm(x, norm_ffn_w, BF16)
    act = swiglu_up(h, w_gate_up)
    return swiglu_down(act, w_down, x)


def kernel(x_prompt, x_sample, norm_mix_w, w_in, lb_fwd, lb_bwd, hg_norm_w, q_norm_w, k_norm_w,
           w_branch_a, w_branch_b, w_out, norm_ffn_w, w_gate_up, w_down, norm_final_w):
    depth = w_in.shape[0]
    lbs_f = jnp.cumsum(jax.nn.softmax(lb_fwd.astype(F32), axis=0), axis=0)
    lbs_b = jnp.cumsum(jax.nn.softmax(lb_bwd.astype(F32), axis=0), axis=0)
    bf = lambda w: w.astype(BF16)
    w_in, w_branch_a, w_branch_b, w_out, w_gate_up, w_down = map(
        bf, (w_in, w_branch_a, w_branch_b, w_out, w_gate_up, w_down))

    def trunk(x):
        x = x.reshape(x.shape[-2], x.shape[-1])
        for l in range(depth):
            x = _encoder_layer(x, norm_mix_w[l], w_in[l], lbs_f[l], lbs_b[l], hg_norm_w[l],
                               q_norm_w[l], k_norm_w[l], w_branch_a[l], w_branch_b[l], w_out[l],
                               norm_ffn_w[l], w_gate_up[l], w_down[l])
        return rmsnorm(x, norm_final_w, F32)

    assert x_prompt.shape[0] == 1 and x_sample.shape[0] == 1
    return trunk(x_prompt)[None], trunk(x_sample)[None]
```

```python
import functools

import jax
import jax.numpy as jnp
import numpy as np
from jax import lax
from jax.experimental import pallas as pl
from jax.experimental.pallas import tpu as pltpu

F32 = jnp.float32
BF16 = jnp.bfloat16

EPS = 1e-6
HEAD = 128
HG_HEADS = 16
ATT_Q_HEADS = 16
ATT_KV_HEADS = 4
GRID_W = 64
ROPE_THETA = 10000.0

V7X_VMEM_LIMIT_BYTES = 56 * 1024 * 1024

HG_CHUNK = 64
HG_TILE = 256
HG_GROUP = 4
HG_MAX_EXPONENT = 115.0

ATT_SUB_K = 512
ATT_Q_SCALE = float(np.log2(np.e) / np.sqrt(np.float32(HEAD)))
ATT_MAX_BOUNDED_LOGIT = 60.0


def _params(*semantics):
    return pltpu.CompilerParams(dimension_semantics=semantics,
                                vmem_limit_bytes=V7X_VMEM_LIMIT_BYTES)


def _sigmoid(x):
    return 0.5 + 0.5 * jnp.tanh(0.5 * x)


def _silu(x):
    return x * _sigmoid(x)


def _rmsnorm_kernel(x_ref, w_ref, o_ref):
    x = x_ref[...]
    inv = lax.rsqrt(jnp.mean(x * x, axis=-1, keepdims=True) + EPS)
    o_ref[...] = (x * inv * w_ref[...]).astype(o_ref.dtype)


def rmsnorm(x, w, out_dtype, tm=256):
    T, D = x.shape
    return pl.pallas_call(
        _rmsnorm_kernel,
        grid=(T // tm,),
        in_specs=[pl.BlockSpec((tm, D), lambda i: (i, 0)),
                  pl.BlockSpec((1, D), lambda i: (0, 0))],
        out_specs=pl.BlockSpec((tm, D), lambda i: (i, 0)),
        out_shape=jax.ShapeDtypeStruct((T, D), out_dtype),
        compiler_params=_params("parallel"),
        name="rmsnorm",
    )(x, w.reshape(1, D))


def _proj_kernel(a_ref, w_ref, o_ref):
    o_ref[...] = jnp.dot(a_ref[...], w_ref[...], preferred_element_type=F32)


def in_projection(h, w, tm=1024, tn=1024):
    T, K = h.shape
    N = w.shape[1]
    return pl.pallas_call(
        _proj_kernel,
        grid=(T // tm, N // tn),
        in_specs=[pl.BlockSpec((tm, K), lambda i, j: (i, 0)),
                  pl.BlockSpec((K, tn), lambda i, j: (0, j))],
        out_specs=pl.BlockSpec((tm, tn), lambda i, j: (i, j)),
        out_shape=jax.ShapeDtypeStruct((T, N), F32),
        compiler_params=_params("parallel", "arbitrary"),
        name="in_projection",
    )(h, w)


def _merge_kernel(oa_ref, ob_ref, wa_ref, wb_ref, ga_ref, gb_ref, o_ref):
    ya = jnp.dot(oa_ref[...], wa_ref[...], preferred_element_type=F32)
    yb = jnp.dot(ob_ref[...], wb_ref[...], preferred_element_type=F32)
    o_ref[...] = (_sigmoid(ga_ref[...]) * ya + _sigmoid(gb_ref[...]) * yb).astype(o_ref.dtype)


def gated_merge(o_hg, o_att, w_a, w_b, proj, ga_col, gb_col, tm=1024, tn=512):
    T, K = o_hg.shape
    D = w_a.shape[1]
    ga_blk, gb_blk = ga_col // tn, gb_col // tn
    return pl.pallas_call(
        _merge_kernel,
        grid=(T // tm, D // tn),
        in_specs=[pl.BlockSpec((tm, K), lambda i, j: (i, 0)),
                  pl.BlockSpec((tm, K), lambda i, j: (i, 0)),
                  pl.BlockSpec((K, tn), lambda i, j: (0, j)),
                  pl.BlockSpec((K, tn), lambda i, j: (0, j)),
                  pl.BlockSpec((tm, tn), lambda i, j: (i, ga_blk + j)),
                  pl.BlockSpec((tm, tn), lambda i, j: (i, gb_blk + j))],
        out_specs=pl.BlockSpec((tm, tn), lambda i, j: (i, j)),
        out_shape=jax.ShapeDtypeStruct((T, D), BF16),
        compiler_params=_params("parallel", "arbitrary"),
        name="gated_merge",
    )(o_hg, o_att, w_a, w_b, proj, proj)


def _residual_mm_kernel(a_ref, w_ref, r_ref, o_ref):
    o_ref[...] = r_ref[...] + jnp.dot(a_ref[...], w_ref[...], preferred_element_type=F32)


def residual_matmul(a, w, res, tm=1024, tn=1024):
    T, K = a.shape
    N = w.shape[1]
    return pl.pallas_call(
        _residual_mm_kernel,
        grid=(T // tm, N // tn),
        in_specs=[pl.BlockSpec((tm, K), lambda i, j: (i, 0)),
                  pl.BlockSpec((K, tn), lambda i, j: (0, j)),
                  pl.BlockSpec((tm, tn), lambda i, j: (i, j))],
        out_specs=pl.BlockSpec((tm, tn), lambda i, j: (i, j)),
        out_shape=jax.ShapeDtypeStruct((T, N), F32),
        compiler_params=_params("parallel", "arbitrary"),
        name="residual_matmul",
    )(a, w, res)


def _swiglu_up_kernel(a_ref, wg_ref, wu_ref, o_ref):
    a = a_ref[...]
    g = jnp.dot(a, wg_ref[...], preferred_element_type=F32)
    u = jnp.dot(a, wu_ref[...], preferred_element_type=F32)
    o_ref[...] = (_silu(g) * u).astype(o_ref.dtype)


def swiglu_up(h, w_gate_up, tm=2048, tn=256):
    T, K = h.shape
    F = w_gate_up.shape[1] // 2
    u_blk = F // tn
    return pl.pallas_call(
        _swiglu_up_kernel,
        grid=(T // tm, F // tn),
        in_specs=[pl.BlockSpec((tm, K), lambda i, j: (i, 0)),
                  pl.BlockSpec((K, tn), lambda i, j: (0, j)),
                  pl.BlockSpec((K, tn), lambda i, j: (0, u_blk + j))],
        out_specs=pl.BlockSpec((tm, tn), lambda i, j: (i, j)),
        out_shape=jax.ShapeDtypeStruct((T, F), BF16),
        compiler_params=_params("parallel", "arbitrary"),
        name="swiglu_up",
    )(h, w_gate_up, w_gate_up)


def _down_kernel(a_ref, w_ref, r_ref, o_ref):
    part = jnp.dot(a_ref[...], w_ref[...], preferred_element_type=F32)

    @pl.when(pl.program_id(2) == 0)
    def _():
        o_ref[...] = r_ref[...] + part

    @pl.when(pl.program_id(2) != 0)
    def _():
        o_ref[...] += part


def swiglu_down(act, w, res, tm=512, tn=512, k_splits=1):
    T, K = act.shape
    N = w.shape[1]
    tk = K // k_splits
    return pl.pallas_call(
        _down_kernel,
        grid=(T // tm, N // tn, k_splits),
        in_specs=[pl.BlockSpec((tm, tk), lambda i, j, k: (i, k)),
                  pl.BlockSpec((tk, tn), lambda i, j, k: (k, j)),
                  pl.BlockSpec((tm, tn), lambda i, j, k: (i, j))],
        out_specs=pl.BlockSpec((tm, tn), lambda i, j, k: (i, j)),
        out_shape=jax.ShapeDtypeStruct((T, N), F32),
        compiler_params=_params("parallel", "arbitrary", "arbitrary"),
        name="swiglu_down",
    )(act, w, res)


def _split2_bf16(x):
    hi = x.astype(BF16)
    return hi, (x - hi.astype(F32)).astype(BF16)


def _forget_gate(z, lb):
    half = 0.5 * (1.0 - lb)
    kk = half - half * jnp.tanh(0.5 * z)
    f = 1.0 - kk
    return f, jnp.log2(f), kk


def _hgrn_scan_tile(q_ref, z_ref, v_ref, lb_ref, st_ref, st_save, o_scr, vt_scr, emit, reverse):
    C = HG_CHUNK
    n_chunks = HG_TILE // C
    chunks = [slice(j * C, (j + 1) * C) for j in range(n_chunks)]
    row = lax.broadcasted_iota(jnp.int32, (HG_TILE, HG_TILE), 0)
    col = lax.broadcasted_iota(jnp.int32, (HG_TILE, HG_TILE), 1)
    visible = (col >= row) if reverse else (col <= row)
    chunk_bits = C.bit_length() - 1
    assert C == 1 << chunk_bits
    same_chunk = lax.shift_right_logical(row, chunk_bits) == lax.shift_right_logical(col, chunk_bits)
    keep = visible & same_chunk
    tri = jnp.where(keep, 1.0, 0.0).astype(BF16)
    end = 0 if reverse else C - 1
    mid = C // 2 if reverse else C // 2 - 1
    heads = [slice(g * HEAD, (g + 1) * HEAD) for g in range(HG_GROUP)]
    nt_dims = (((1,), (1,)), ((), ()))

    def rel(c, off):
        return jnp.concatenate([c[sl] - c[sl.start + off:sl.start + off + 1] for sl in chunks],
                               axis=0)

    st_save[...] = st_ref[...]

    keys, cums = [], []
    for hs in heads:
        _, logf, kk = _forget_gate(z_ref[:, hs], lb_ref[:, hs])
        keys.append(kk)
        cs = jnp.dot(tri, jnp.concatenate(_split2_bf16(logf), axis=1),
                     preferred_element_type=F32)
        cums.append(cs[:, :HEAD] + cs[:, HEAD:])
    mids = [rel(c, mid) for c in cums]
    worst = functools.reduce(jnp.maximum, [jnp.max(jnp.abs(a)) for a in mids])
    scores = []
    for g, hs in enumerate(heads):
        q_in = (q_ref[:, hs] * jnp.exp2(mids[g])).astype(BF16)
        k_in = (keys[g] * jnp.exp2(-mids[g])).astype(BF16)
        scores.append(lax.dot_general(q_in, k_in, nt_dims, preferred_element_type=F32))
    q_state, decay, update = [], {}, {}
    for g, hs in enumerate(heads):
        c, v = cums[g], v_ref[:, hs]
        k_st = (keys[g] * jnp.exp2(-rel(c, end))).astype(BF16)
        q_state.append((q_ref[:, hs] * jnp.exp2(c)).astype(BF16))
        for j, sl in enumerate(chunks):
            update[g, j] = jnp.dot(v[sl].T.astype(BF16), k_st[sl], preferred_element_type=F32)
            decay[g, j] = jnp.exp2(c[sl.start + end:sl.start + end + 1])
    for g, hs in enumerate(heads):
        s = jnp.where(keep, scores[g], 0.0).astype(BF16)
        o_scr[:, hs] = jnp.dot(s, v_ref[:, hs].astype(BF16), preferred_element_type=F32)
    order = range(n_chunks - 1, -1, -1) if reverse else range(n_chunks)
    for j in order:
        sl = chunks[j]
        for g, hs in enumerate(heads):
            st = st_ref[g]
            o_scr[sl, hs] += lax.dot_general(q_state[g][sl], st.astype(BF16), nt_dims,
                                             preferred_element_type=F32)
            st_ref[g] = st * decay[g, j] + update[g, j]
    emit()

    @pl.when(jnp.logical_not(worst < HG_MAX_EXPONENT))
    def _():
        st_ref[...] = st_save[...]
        lane = lax.broadcasted_iota(jnp.int32, (1, HG_TILE), 1)
        rows = lax.broadcasted_iota(jnp.int32, (HG_TILE, 1), 0)
        for g, hs in enumerate(heads):
            z, lb = z_ref[:, hs], lb_ref[:, hs]
            q = q_ref[:, hs]
            vt_scr[...] = v_ref[:, hs].T

            def body(i, carry):
                t = (HG_TILE - 1 - i) if reverse else i
                onehot = (lane == t).astype(F32)
                pick = lambda x: jnp.sum(jnp.where(rows == t, x, 0.0), axis=0, keepdims=True)
                f_t, _, k_t = _forget_gate(pick(z), lb)
                q_t = pick(q)
                v_col = jnp.sum(vt_scr[...] * onehot, axis=1, keepdims=True)
                st = st_ref[g] * f_t + v_col * k_t
                st_ref[g] = st
                o_col = jnp.sum(st * q_t, axis=1, keepdims=True)
                return carry + o_col * onehot

            ot = lax.fori_loop(0, HG_TILE, body, jnp.zeros((HEAD, HG_TILE), F32))
            o_scr[:, hs] = ot.T
        emit()


def _hgrn_fwd_kernel(q_ref, z_ref, v_ref, lb_ref, o_ref, st_ref, st_save, o_scr, vt_scr):
    @pl.when(pl.program_id(1) == 0)
    def _():
        st_ref[...] = jnp.zeros_like(st_ref)

    def emit():
        o_ref[...] = o_scr[...]

    _hgrn_scan_tile(q_ref, z_ref, v_ref, lb_ref, st_ref, st_save, o_scr, vt_scr, emit,
                    reverse=False)


def _hgrn_bwd_kernel(q_ref, z_ref, v_ref, lb_ref, of_ref, gate_ref, nw_ref, o_ref,
                     st_ref, st_save, o_scr, vt_scr):
    @pl.when(pl.program_id(1) == 0)
    def _():
        st_ref[...] = jnp.zeros_like(st_ref)

    def emit():
        for g in range(HG_GROUP):
            hs = slice(g * HEAD, (g + 1) * HEAD)
            o = of_ref[:, hs] + o_scr[:, hs]
            o = o * lax.rsqrt(jnp.mean(o * o, axis=-1, keepdims=True) + EPS) * nw_ref[:, hs]
            o_ref[:, hs] = (o * _silu(gate_ref[:, hs])).astype(o_ref.dtype)

    _hgrn_scan_tile(q_ref, z_ref, v_ref, lb_ref, st_ref, st_save, o_scr, vt_scr, emit,
                    reverse=True)


def hgrn2_bidirectional(proj, lb_f, lb_b, norm_w, cols):
    T = proj.shape[0]
    n_tiles = T // HG_TILE
    W = HG_HEADS * HEAD
    GW = HG_GROUP * HEAD
    n_groups = HG_HEADS // HG_GROUP
    cq, czf, czb, ci, cg = (c // GW for c in cols)
    scratch = [pltpu.VMEM((HG_GROUP, HEAD, HEAD), F32),
               pltpu.VMEM((HG_GROUP, HEAD, HEAD), F32),
               pltpu.VMEM((HG_TILE, GW), F32),
               pltpu.VMEM((HEAD, HG_TILE), F32)]
    tile = lambda blk: pl.BlockSpec((HG_TILE, GW), lambda h, i: (i, blk + h))
    vec = pl.BlockSpec((1, GW), lambda h, i: (0, h))
    o_fwd = pl.pallas_call(
        _hgrn_fwd_kernel,
        grid=(n_groups, n_tiles),
        in_specs=[tile(cq), tile(czf), tile(ci), vec],
        out_specs=pl.BlockSpec((HG_TILE, GW), lambda h, i: (i, h)),
        out_shape=jax.ShapeDtypeStruct((T, W), F32),
        scratch_shapes=scratch,
        compiler_params=_params("parallel", "arbitrary"),
        name="hgrn_fwd",
    )(proj, proj, proj, lb_f.reshape(1, W))

    last = n_tiles - 1
    rtile = lambda blk: pl.BlockSpec((HG_TILE, GW), lambda h, i: (last - i, blk + h))
    return pl.pallas_call(
        _hgrn_bwd_kernel,
        grid=(n_groups, n_tiles),
        in_specs=[rtile(cq), rtile(czb), rtile(ci), vec, rtile(0), rtile(cg), vec],
        out_specs=pl.BlockSpec((HG_TILE, GW), lambda h, i: (last - i, h)),
        out_shape=jax.ShapeDtypeStruct((T, W), BF16),
        scratch_shapes=scratch,
        compiler_params=_params("parallel", "arbitrary"),
        name="hgrn_bwd",
    )(proj, proj, proj, lb_b.reshape(1, W), o_fwd, proj, norm_w.reshape(1, W))


def _rope_tables(T):
    rows = T // GRID_W
    row = jnp.repeat(jnp.arange(rows, dtype=F32), GRID_W)
    col = jnp.tile(jnp.arange(GRID_W, dtype=F32), rows)
    axis_dim = HEAD // 2
    inv = ROPE_THETA ** (-jnp.arange(0, axis_dim, 2, dtype=F32) / axis_dim)
    ang = jnp.concatenate([row[:, None] * inv, col[:, None] * inv], axis=-1)
    cos = jnp.repeat(jnp.cos(ang), 2, axis=-1)
    sin = jnp.repeat(jnp.sin(ang), 2, axis=-1)
    sign = jnp.tile(jnp.array([-1.0, 1.0], F32), HEAD // 2)
    return cos, sin * sign


def _norm_rope_head(x, w, cos, sin_signed, scale):
    x = x * lax.rsqrt(jnp.mean(x * x, axis=-1, keepdims=True) + EPS) * w
    lane = lax.broadcasted_iota(jnp.int32, x.shape, 1)
    partner = jnp.where(lane % 2 == 0,
                        pltpu.roll(x, HEAD - 1, axis=1),
                        pltpu.roll(x, 1, axis=1))
    out = x * cos + partner * sin_signed
    return out * scale if scale is not None else out


def _qk_rope_kernel(q_ref, k_ref, v_ref, qw_ref, kw_ref, cos_ref, sin_ref,
                    qt_ref, ko_ref, vt_ref):
    cos, sin = cos_ref[...], sin_ref[...]
    for h in range(ATT_Q_HEADS):
        sl = slice(h * HEAD, (h + 1) * HEAD)
        q = _norm_rope_head(q_ref[:, sl], qw_ref[...], cos, sin, ATT_Q_SCALE)
        qt_ref[sl, :] = q.T.astype(qt_ref.dtype)
    for h in range(ATT_KV_HEADS):
        sl = slice(h * HEAD, (h + 1) * HEAD)
        ko_ref[:, sl] = _norm_rope_head(k_ref[:, sl], kw_ref[...], cos, sin,
                                        None).astype(ko_ref.dtype)
        vt_ref[0, sl, :] = v_ref[:, sl].T.astype(vt_ref.dtype)


def qk_norm_rope(proj, q_w, k_w, cols):
    T = proj.shape[0]
    tm = ATT_SUB_K
    QW, KW = ATT_Q_HEADS * HEAD, ATT_KV_HEADS * HEAD
    cq, ck, cv = cols
    cos, sin = _rope_tables(T)
    return pl.pallas_call(
        _qk_rope_kernel,
        grid=(T // tm,),
        in_specs=[pl.BlockSpec((tm, QW), lambda i: (i, cq // QW)),
                  pl.BlockSpec((tm, KW), lambda i: (i, ck // KW)),
                  pl.BlockSpec((tm, KW), lambda i: (i, cv // KW)),
                  pl.BlockSpec((1, HEAD), lambda i: (0, 0)),
                  pl.BlockSpec((1, HEAD), lambda i: (0, 0)),
                  pl.BlockSpec((tm, HEAD), lambda i: (i, 0)),
                  pl.BlockSpec((tm, HEAD), lambda i: (i, 0))],
        out_specs=[pl.BlockSpec((QW, tm), lambda i: (0, i)),
                   pl.BlockSpec((tm, KW), lambda i: (i, 0)),
                   pl.BlockSpec((1, KW, tm), lambda i: (i, 0, 0))],
        out_shape=[jax.ShapeDtypeStruct((QW, T), BF16),
                   jax.ShapeDtypeStruct((T, KW), BF16),
                   jax.ShapeDtypeStruct((T // tm, KW, tm), BF16)],
        compiler_params=_params("parallel"),
        name="qk_norm_rope",
    )(proj, proj, proj, q_w.reshape(1, HEAD), k_w.reshape(1, HEAD), cos, sin)


def _row_norm_bound(norm_w, scale):
    margin = 1.0 + 2.0 ** -7
    return jnp.sqrt(jnp.float32(HEAD)) * jnp.max(jnp.abs(norm_w)) * (scale * margin)


def _flash_kernel(bounded_ref, qt_ref, k_ref, vt_ref, kmax_ref, o_ref, m_scr, l_scr, acc_scr):
    j = pl.program_id(2)
    group = ATT_Q_HEADS // ATT_KV_HEADS
    n_sub = k_ref.shape[0] // ATT_SUB_K
    bounded = bounded_ref[0] != 0

    def key_block(jj):
        k = k_ref[pl.ds(pl.multiple_of(jj * ATT_SUB_K, ATT_SUB_K), ATT_SUB_K), :]
        return k, vt_ref[jj]

    def logits(k, r):
        return jnp.dot(k, qt_ref[r * HEAD:(r + 1) * HEAD, :], preferred_element_type=F32)

    @pl.when(j == 0)
    def _():
        l_scr[...] = jnp.zeros_like(l_scr)
        acc_scr[...] = jnp.zeros_like(acc_scr)

        @pl.when(bounded)
        def _():
            for r in range(group):
                q = qt_ref[r * HEAD:(r + 1) * HEAD, :].astype(F32)
                m_scr[r] = jnp.sqrt(jnp.sum(q * q, axis=0, keepdims=True)) * kmax_ref[...]

        @pl.when(jnp.logical_not(bounded))
        def _():
            m_scr[...] = jnp.full_like(m_scr, -jnp.inf)

    @pl.when(bounded)
    def _():
        stages = [(jj, r) for jj in range(n_sub) for r in range(group)]

        def stage_logits(stage):
            jj, r = stage
            return logits(k_ref[jj * ATT_SUB_K:(jj + 1) * ATT_SUB_K, :], r)

        st = stage_logits(stages[0])
        for idx, (jj, r) in enumerate(stages):
            st_next = stage_logits(stages[idx + 1]) if idx + 1 < len(stages) else None
            pt = jnp.exp2(st - m_scr[r])
            l_scr[r] += jnp.sum(pt, axis=0, keepdims=True)
            acc_scr[r] += jnp.dot(vt_ref[jj], pt.astype(BF16), preferred_element_type=F32)
            st = st_next

    @pl.when(jnp.logical_not(bounded))
    def _():
        def sub_block(jj, carry):
            k, vt = key_block(jj)
            for r in range(group):
                st = logits(k, r)
                m_prev = m_scr[r]
                m_new = jnp.maximum(m_prev, jnp.max(st, axis=0, keepdims=True))
                alpha = jnp.exp2(m_prev - m_new)
                pt = jnp.exp2(st - m_new)
                l_scr[r] = alpha * l_scr[r] + jnp.sum(pt, axis=0, keepdims=True)
                acc_scr[r] = alpha * acc_scr[r] + jnp.dot(vt, pt.astype(BF16),
                                                          preferred_element_type=F32)
                m_scr[r] = m_new
            return carry

        lax.fori_loop(0, n_sub, sub_block, 0)

    @pl.when(j == pl.num_programs(2) - 1)
    def _():
        for r in range(group):
            o = acc_scr[r] / l_scr[r]
            o_ref[:, r * HEAD:(r + 1) * HEAD] = o.T.astype(o_ref.dtype)


def flash_attention(qt, k, vt, q_max, k_max, tq=512, tk=4096):
    T = k.shape[0]
    group = ATT_Q_HEADS // ATT_KV_HEADS
    GW = group * HEAD
    bounded = (q_max * k_max <= ATT_MAX_BOUNDED_LOGIT).astype(jnp.int32).reshape(1)
    kmax_row = jnp.full((1, tq), k_max, F32)
    grid_spec = pltpu.PrefetchScalarGridSpec(
        num_scalar_prefetch=1,
        grid=(ATT_KV_HEADS, T // tq, T // tk),
        in_specs=[pl.BlockSpec((GW, tq), lambda g, i, j, b: (g, i)),
                  pl.BlockSpec((tk, HEAD), lambda g, i, j, b: (j, g)),
                  pl.BlockSpec((tk // ATT_SUB_K, HEAD, ATT_SUB_K), lambda g, i, j, b: (j, g, 0)),
                  pl.BlockSpec((1, tq), lambda g, i, j, b: (0, 0))],
        out_specs=pl.BlockSpec((tq, GW), lambda g, i, j, b: (i, g)),
        scratch_shapes=[pltpu.VMEM((group, 1, tq), F32),
                        pltpu.VMEM((group, 1, tq), F32),
                        pltpu.VMEM((group, HEAD, tq), F32)])
    return pl.pallas_call(
        _flash_kernel,
        grid_spec=grid_spec,
        out_shape=jax.ShapeDtypeStruct((T, ATT_Q_HEADS * HEAD), BF16),
        compiler_params=_params("parallel", "parallel", "arbitrary"),
        name="flash_attention",
    )(bounded, qt, k, vt, kmax_row)


def _encoder_layer(x, norm_mix_w, w_in, lb_f, lb_b, hg_norm_w, q_norm_w, k_norm_w,
                   w_branch_a, w_branch_b, w_out, norm_ffn_w, w_gate_up, w_down):
    hg_w = HG_HEADS * HEAD
    q_w, kv_w = ATT_Q_HEADS * HEAD, ATT_KV_HEADS * HEAD
    D = x.shape[1]
    sizes = (hg_w, hg_w, hg_w, hg_w, hg_w, q_w, kv_w, kv_w, D, D)
    offs = np.concatenate([[0], np.cumsum(sizes)]).tolist()
    assert offs[-1] == w_in.shape[1]

    h = rmsnorm(x, norm_mix_w, BF16)
    proj = in_projection(h, w_in)
    o_hg = hgrn2_bidirectional(proj, lb_f, lb_b, hg_norm_w, offs[0:5])
    qt, k, vt = qk_norm_rope(proj, q_norm_w, k_norm_w, offs[5:8])
    o_att = flash_attention(qt, k, vt, _row_norm_bound(q_norm_w, ATT_Q_SCALE),
                            _row_norm_bound(k_norm_w, 1.0))
    merged = gated_merge(o_hg, o_att, w_branch_a, w_branch_b, proj, offs[8], offs[9])
    x = residual_matmul(merged, w_out, x)
    h = rmsnorm(x, norm_ffn_w, BF16)
    act = swiglu_up(h, w_gate_up)
    return swiglu_down(act, w_down, x)


def kernel(x_prompt, x_sample, norm_mix_w, w_in, lb_fwd, lb_bwd, hg_norm_w, q_norm_w, k_norm_w,
           w_branch_a, w_branch_b, w_out, norm_ffn_w, w_gate_up, w_down, norm_final_w):
    depth = w_in.shape[0]
    lbs_f = jnp.cumsum(jax.nn.softmax(lb_fwd.astype(F32), axis=0), axis=0)
    lbs_b = jnp.cumsum(jax.nn.softmax(lb_bwd.astype(F32), axis=0), axis=0)
    bf = lambda w: w.astype(BF16)
    w_in, w_branch_a, w_branch_b, w_out, w_gate_up, w_down = map(
        bf, (w_in, w_branch_a, w_branch_b, w_out, w_gate_up, w_down))

    def trunk(x):
        x = x.reshape(x.shape[-2], x.shape[-1])
        for l in range(depth):
            x = _encoder_layer(x, norm_mix_w[l], w_in[l], lbs_f[l], lbs_b[l], hg_norm_w[l],
                               q_norm_w[l], k_norm_w[l], w_branch_a[l], w_branch_b[l], w_out[l],
                               norm_ffn_w[l], w_gate_up[l], w_down[l])
        return rmsnorm(x, norm_final_w, F32)

    assert x_prompt.shape[0] == 1 and x_sample.shape[0] == 1
    return trunk(x_prompt)[None], trunk(x_sample)[None]
```

```python
import functools

import jax
import jax.numpy as jnp
import numpy as np
from jax import lax
from jax.experimental import pallas as pl
from jax.experimental.pallas import tpu as pltpu

F32 = jnp.float32
BF16 = jnp.bfloat16

EPS = 1e-6
LANES = 128
HEAD = 128
HG_HEADS = 16
ATT_Q_HEADS = 16
ATT_KV_HEADS = 4
GRID_W = 64
ROPE_THETA = 10000.0

V7X_VMEM_LIMIT_BYTES = 56 * 1024 * 1024

HG_CHUNK = 64
HG_TILE = 256
HG_GROUP = 4
HG_MAX_EXPONENT = 115.0

ATT_SUB_K = 512
ATT_Q_SCALE = float(np.log2(np.e) / np.sqrt(np.float32(HEAD)))
ATT_MAX_BOUNDED_LOGIT = 60.0


def _params(*semantics):
    return pltpu.CompilerParams(dimension_semantics=semantics,
                                vmem_limit_bytes=V7X_VMEM_LIMIT_BYTES)


def _sigmoid(x):
    return 0.5 + 0.5 * jnp.tanh(0.5 * x)


def _silu(x):
    return x * _sigmoid(x)


def _rmsnorm_kernel(x_ref, w_ref, o_ref):
    x = x_ref[...]
    inv = lax.rsqrt(jnp.mean(x * x, axis=-1, keepdims=True) + EPS)
    o_ref[...] = (x * inv * w_ref[...]).astype(o_ref.dtype)


def rmsnorm(x, w, out_dtype, tm=256):
    T, D = x.shape
    return pl.pallas_call(
        _rmsnorm_kernel,
        grid=(T // tm,),
        in_specs=[pl.BlockSpec((tm, D), lambda i: (i, 0)),
                  pl.BlockSpec((1, D), lambda i: (0, 0))],
        out_specs=pl.BlockSpec((tm, D), lambda i: (i, 0)),
        out_shape=jax.ShapeDtypeStruct((T, D), out_dtype),
        compiler_params=_params("parallel"),
        name="rmsnorm",
    )(x, w.reshape(1, D))


def _proj_kernel(a_ref, w_ref, o_ref):
    o_ref[...] = jnp.dot(a_ref[...], w_ref[...], preferred_element_type=F32)


def in_projection(h, w, tm=1024, tn=1024):
    T, K = h.shape
    N = w.shape[1]
    return pl.pallas_call(
        _proj_kernel,
        grid=(T // tm, N // tn),
        in_specs=[pl.BlockSpec((tm, K), lambda i, j: (i, 0)),
                  pl.BlockSpec((K, tn), lambda i, j: (0, j))],
        out_specs=pl.BlockSpec((tm, tn), lambda i, j: (i, j)),
        out_shape=jax.ShapeDtypeStruct((T, N), F32),
        compiler_params=_params("parallel", "arbitrary"),
        name="in_projection",
    )(h, w)


def _merge_kernel(oa_ref, ob_ref, wa_ref, wb_ref, ga_ref, gb_ref, o_ref):
    ya = jnp.dot(oa_ref[...], wa_ref[...], preferred_element_type=F32)
    yb = jnp.dot(ob_ref[...], wb_ref[...], preferred_element_type=F32)
    o_ref[...] = (_sigmoid(ga_ref[...]) * ya + _sigmoid(gb_ref[...]) * yb).astype(o_ref.dtype)


def gated_merge(o_hg, o_att, w_a, w_b, proj, ga_col, gb_col, tm=1024, tn=512):
    T, K = o_hg.shape
    D = w_a.shape[1]
    ga_blk, gb_blk = ga_col // tn, gb_col // tn
    return pl.pallas_call(
        _merge_kernel,
        grid=(T // tm, D // tn),
        in_specs=[pl.BlockSpec((tm, K), lambda i, j: (i, 0)),
                  pl.BlockSpec((tm, K), lambda i, j: (i, 0)),
                  pl.BlockSpec((K, tn), lambda i, j: (0, j)),
                  pl.BlockSpec((K, tn), lambda i, j: (0, j)),
                  pl.BlockSpec((tm, tn), lambda i, j: (i, ga_blk + j)),
                  pl.BlockSpec((tm, tn), lambda i, j: (i, gb_blk + j))],
        out_specs=pl.BlockSpec((tm, tn), lambda i, j: (i, j)),
        out_shape=jax.ShapeDtypeStruct((T, D), BF16),
        compiler_params=_params("parallel", "arbitrary"),
        name="gated_merge",
    )(o_hg, o_att, w_a, w_b, proj, proj)


def _residual_mm_kernel(a_ref, w_ref, r_ref, o_ref, ob_ref, ss_ref):
    y = r_ref[...] + jnp.dot(a_ref[...], w_ref[...], preferred_element_type=F32)
    o_ref[...] = y
    ob_ref[...] = y.astype(ob_ref.dtype)
    sq = y * y
    part = functools.reduce(
        jnp.add, [sq[:, c * LANES:(c + 1) * LANES] for c in range(sq.shape[1] // LANES)])

    @pl.when(pl.program_id(1) == 0)
    def _():
        ss_ref[...] = part

    @pl.when(pl.program_id(1) != 0)
    def _():
        ss_ref[...] += part


def residual_matmul(a, w, res, tm=1024, tn=512):
    T, K = a.shape
    N = w.shape[1]
    return pl.pallas_call(
        _residual_mm_kernel,
        grid=(T // tm, N // tn),
        in_specs=[pl.BlockSpec((tm, K), lambda i, j: (i, 0)),
                  pl.BlockSpec((K, tn), lambda i, j: (0, j)),
                  pl.BlockSpec((tm, tn), lambda i, j: (i, j))],
        out_specs=[pl.BlockSpec((tm, tn), lambda i, j: (i, j)),
                   pl.BlockSpec((tm, tn), lambda i, j: (i, j)),
                   pl.BlockSpec((tm, LANES), lambda i, j: (i, 0))],
        out_shape=[jax.ShapeDtypeStruct((T, N), F32),
                   jax.ShapeDtypeStruct((T, N), BF16),
                   jax.ShapeDtypeStruct((T, LANES), F32)],
        compiler_params=_params("parallel", "arbitrary"),
        name="residual_matmul",
    )(a, w, res)


def _swiglu_up_kernel(a_ref, ss_ref, wg_ref, wu_ref, o_ref, inv_scr):
    @pl.when(pl.program_id(1) == 0)
    def _():
        mean_sq = jnp.sum(ss_ref[...], axis=-1, keepdims=True) * (1.0 / a_ref.shape[1])
        inv_scr[...] = jnp.broadcast_to(lax.rsqrt(mean_sq + EPS), inv_scr.shape)

    a = a_ref[...]
    inv = jnp.concatenate([inv_scr[...]] * (o_ref.shape[1] // LANES), axis=1)
    g = jnp.dot(a, wg_ref[...], preferred_element_type=F32) * inv
    u = jnp.dot(a, wu_ref[...], preferred_element_type=F32) * inv
    o_ref[...] = (_silu(g) * u).astype(o_ref.dtype)


def swiglu_up(xb, sumsq, w_gate_up, tm=2048, tn=256):
    T, K = xb.shape
    F = w_gate_up.shape[1] // 2
    u_blk = F // tn
    return pl.pallas_call(
        _swiglu_up_kernel,
        grid=(T // tm, F // tn),
        in_specs=[pl.BlockSpec((tm, K), lambda i, j: (i, 0)),
                  pl.BlockSpec((tm, LANES), lambda i, j: (i, 0)),
                  pl.BlockSpec((K, tn), lambda i, j: (0, j)),
                  pl.BlockSpec((K, tn), lambda i, j: (0, u_blk + j))],
        out_specs=pl.BlockSpec((tm, tn), lambda i, j: (i, j)),
        out_shape=jax.ShapeDtypeStruct((T, F), BF16),
        scratch_shapes=[pltpu.VMEM((tm, LANES), F32)],
        compiler_params=_params("parallel", "arbitrary"),
        name="swiglu_up",
    )(xb, sumsq, w_gate_up, w_gate_up)


def _down_kernel(a_ref, w_ref, r_ref, o_ref):
    part = jnp.dot(a_ref[...], w_ref[...], preferred_element_type=F32)

    @pl.when(pl.program_id(2) == 0)
    def _():
        o_ref[...] = r_ref[...] + part

    @pl.when(pl.program_id(2) != 0)
    def _():
        o_ref[...] += part


def swiglu_down(act, w, res, tm=512, tn=512, k_splits=1):
    T, K = act.shape
    N = w.shape[1]
    tk = K // k_splits
    return pl.pallas_call(
        _down_kernel,
        grid=(T // tm, N // tn, k_splits),
        in_specs=[pl.BlockSpec((tm, tk), lambda i, j, k: (i, k)),
                  pl.BlockSpec((tk, tn), lambda i, j, k: (k, j)),
                  pl.BlockSpec((tm, tn), lambda i, j, k: (i, j))],
        out_specs=pl.BlockSpec((tm, tn), lambda i, j, k: (i, j)),
        out_shape=jax.ShapeDtypeStruct((T, N), F32),
        compiler_params=_params("parallel", "arbitrary", "arbitrary"),
        name="swiglu_down",
    )(act, w, res)


def _split2_bf16(x):
    hi = x.astype(BF16)
    return hi, (x - hi.astype(F32)).astype(BF16)


def _forget_gate(z, lb):
    half = 0.5 * (1.0 - lb)
    kk = half - half * jnp.tanh(0.5 * z)
    f = 1.0 - kk
    return f, jnp.log2(f), kk


def _hgrn_scan_tile(q_ref, z_ref, v_ref, lb_ref, st_ref, st_save, o_scr, vt_scr, emit, reverse):
    C = HG_CHUNK
    n_chunks = HG_TILE // C
    chunks = [slice(j * C, (j + 1) * C) for j in range(n_chunks)]
    row = lax.broadcasted_iota(jnp.int32, (HG_TILE, HG_TILE), 0)
    col = lax.broadcasted_iota(jnp.int32, (HG_TILE, HG_TILE), 1)
    visible = (col >= row) if reverse else (col <= row)
    chunk_bits = C.bit_length() - 1
    assert C == 1 << chunk_bits
    same_chunk = lax.shift_right_logical(row, chunk_bits) == lax.shift_right_logical(col, chunk_bits)
    keep = visible & same_chunk
    tri = jnp.where(keep, 1.0, 0.0).astype(BF16)
    end = 0 if reverse else C - 1
    mid = C // 2 if reverse else C // 2 - 1
    heads = [slice(g * HEAD, (g + 1) * HEAD) for g in range(HG_GROUP)]
    nt_dims = (((1,), (1,)), ((), ()))

    def rel(c, off):
        return jnp.concatenate([c[sl] - c[sl.start + off:sl.start + off + 1] for sl in chunks],
                               axis=0)

    st_save[...] = st_ref[...]

    keys, cums = [], []
    for hs in heads:
        _, logf, kk = _forget_gate(z_ref[:, hs], lb_ref[:, hs])
        keys.append(kk)
        cs = jnp.dot(tri, jnp.concatenate(_split2_bf16(logf), axis=1),
                     preferred_element_type=F32)
        cums.append(cs[:, :HEAD] + cs[:, HEAD:])
    mids = [rel(c, mid) for c in cums]
    worst = functools.reduce(jnp.maximum, [jnp.max(jnp.abs(a)) for a in mids])
    scores = []
    for g, hs in enumerate(heads):
        q_in = (q_ref[:, hs] * jnp.exp2(mids[g])).astype(BF16)
        k_in = (keys[g] * jnp.exp2(-mids[g])).astype(BF16)
        scores.append(lax.dot_general(q_in, k_in, nt_dims, preferred_element_type=F32))
    q_state, decay, update = [], {}, {}
    for g, hs in enumerate(heads):
        c, v = cums[g], v_ref[:, hs]
        k_st = (keys[g] * jnp.exp2(-rel(c, end))).astype(BF16)
        q_state.append((q_ref[:, hs] * jnp.exp2(c)).astype(BF16))
        for j, sl in enumerate(chunks):
            update[g, j] = jnp.dot(v[sl].T.astype(BF16), k_st[sl], preferred_element_type=F32)
            decay[g, j] = jnp.exp2(c[sl.start + end:sl.start + end + 1])
    for g, hs in enumerate(heads):
        s = jnp.where(keep, scores[g], 0.0).astype(BF16)
        o_scr[:, hs] = jnp.dot(s, v_ref[:, hs].astype(BF16), preferred_element_type=F32)
    order = range(n_chunks - 1, -1, -1) if reverse else range(n_chunks)
    for j in order:
        sl = chunks[j]
        for g, hs in enumerate(heads):
            st = st_ref[g]
            o_scr[sl, hs] += lax.dot_general(q_state[g][sl], st.astype(BF16), nt_dims,
                                             preferred_element_type=F32)
            st_ref[g] = st * decay[g, j] + update[g, j]
    emit()

    @pl.when(jnp.logical_not(worst < HG_MAX_EXPONENT))
    def _():
        st_ref[...] = st_save[...]
        lane = lax.broadcasted_iota(jnp.int32, (1, HG_TILE), 1)
        rows = lax.broadcasted_iota(jnp.int32, (HG_TILE, 1), 0)
        for g, hs in enumerate(heads):
            z, lb = z_ref[:, hs], lb_ref[:, hs]
            q = q_ref[:, hs]
            vt_scr[...] = v_ref[:, hs].T

            def body(i, carry):
                t = (HG_TILE - 1 - i) if reverse else i
                onehot = (lane == t).astype(F32)
                pick = lambda x: jnp.sum(jnp.where(rows == t, x, 0.0), axis=0, keepdims=True)
                f_t, _, k_t = _forget_gate(pick(z), lb)
                q_t = pick(q)
                v_col = jnp.sum(vt_scr[...] * onehot, axis=1, keepdims=True)
                st = st_ref[g] * f_t + v_col * k_t
                st_ref[g] = st
                o_col = jnp.sum(st * q_t, axis=1, keepdims=True)
                return carry + o_col * onehot

            ot = lax.fori_loop(0, HG_TILE, body, jnp.zeros((HEAD, HG_TILE), F32))
            o_scr[:, hs] = ot.T
        emit()


def _hgrn_fwd_kernel(q_ref, z_ref, v_ref, lb_ref, o_ref, st_ref, st_save, o_scr, vt_scr):
    @pl.when(pl.program_id(1) == 0)
    def _():
        st_ref[...] = jnp.zeros_like(st_ref)

    def emit():
        o_ref[...] = o_scr[...]

    _hgrn_scan_tile(q_ref, z_ref, v_ref, lb_ref, st_ref, st_save, o_scr, vt_scr, emit,
                    reverse=False)


def _hgrn_bwd_kernel(q_ref, z_ref, v_ref, lb_ref, of_ref, gate_ref, nw_ref, o_ref,
                     st_ref, st_save, o_scr, vt_scr):
    @pl.when(pl.program_id(1) == 0)
    def _():
        st_ref[...] = jnp.zeros_like(st_ref)

    def emit():
        for g in range(HG_GROUP):
            hs = slice(g * HEAD, (g + 1) * HEAD)
            o = of_ref[:, hs] + o_scr[:, hs]
            o = o * lax.rsqrt(jnp.mean(o * o, axis=-1, keepdims=True) + EPS) * nw_ref[:, hs]
            o_ref[:, hs] = (o * _silu(gate_ref[:, hs])).astype(o_ref.dtype)

    _hgrn_scan_tile(q_ref, z_ref, v_ref, lb_ref, st_ref, st_save, o_scr, vt_scr, emit,
                    reverse=True)


def hgrn2_bidirectional(proj, lb_f, lb_b, norm_w, cols):
    T = proj.shape[0]
    n_tiles = T // HG_TILE
    W = HG_HEADS * HEAD
    GW = HG_GROUP * HEAD
    n_groups = HG_HEADS // HG_GROUP
    cq, czf, czb, ci, cg = (c // GW for c in cols)
    scratch = [pltpu.VMEM((HG_GROUP, HEAD, HEAD), F32),
               pltpu.VMEM((HG_GROUP, HEAD, HEAD), F32),
               pltpu.VMEM((HG_TILE, GW), F32),
               pltpu.VMEM((HEAD, HG_TILE), F32)]
    tile = lambda blk: pl.BlockSpec((HG_TILE, GW), lambda h, i: (i, blk + h))
    vec = pl.BlockSpec((1, GW), lambda h, i: (0, h))
    o_fwd = pl.pallas_call(
        _hgrn_fwd_kernel,
        grid=(n_groups, n_tiles),
        in_specs=[tile(cq), tile(czf), tile(ci), vec],
        out_specs=pl.BlockSpec((HG_TILE, GW), lambda h, i: (i, h)),
        out_shape=jax.ShapeDtypeStruct((T, W), F32),
        scratch_shapes=scratch,
        compiler_params=_params("parallel", "arbitrary"),
        name="hgrn_fwd",
    )(proj, proj, proj, lb_f.reshape(1, W))

    last = n_tiles - 1
    rtile = lambda blk: pl.BlockSpec((HG_TILE, GW), lambda h, i: (last - i, blk + h))
    return pl.pallas_call(
        _hgrn_bwd_kernel,
        grid=(n_groups, n_tiles),
        in_specs=[rtile(cq), rtile(czb), rtile(ci), vec, rtile(0), rtile(cg), vec],
        out_specs=pl.BlockSpec((HG_TILE, GW), lambda h, i: (last - i, h)),
        out_shape=jax.ShapeDtypeStruct((T, W), BF16),
        scratch_shapes=scratch,
        compiler_params=_params("parallel", "arbitrary"),
        name="hgrn_bwd",
    )(proj, proj, proj, lb_b.reshape(1, W), o_fwd, proj, norm_w.reshape(1, W))


def _rope_tables(T):
    rows = T // GRID_W
    row = jnp.repeat(jnp.arange(rows, dtype=F32), GRID_W)
    col = jnp.tile(jnp.arange(GRID_W, dtype=F32), rows)
    axis_dim = HEAD // 2
    inv = ROPE_THETA ** (-jnp.arange(0, axis_dim, 2, dtype=F32) / axis_dim)
    ang = jnp.concatenate([row[:, None] * inv, col[:, None] * inv], axis=-1)
    cos = jnp.repeat(jnp.cos(ang), 2, axis=-1)
    sin = jnp.repeat(jnp.sin(ang), 2, axis=-1)
    sign = jnp.tile(jnp.array([-1.0, 1.0], F32), HEAD // 2)
    return cos, sin * sign


def _norm_rope_head(x, w, cos, sin_signed, scale, axis):
    x = x * lax.rsqrt(jnp.mean(x * x, axis=axis, keepdims=True) + EPS) * w
    pos = lax.broadcasted_iota(jnp.int32, x.shape, axis)
    partner = jnp.where(pos % 2 == 0,
                        pltpu.roll(x, HEAD - 1, axis=axis),
                        pltpu.roll(x, 1, axis=axis))
    out = x * cos + partner * sin_signed
    return out * scale if scale is not None else out


def _qk_rope_kernel(q_ref, k_ref, v_ref, qw_ref, kw_ref, cos_ref, sin_ref, cos_t_ref, sin_t_ref,
                    qt_ref, ko_ref, vt_ref):
    cos_t, sin_t = cos_t_ref[...], sin_t_ref[...]
    for h in range(ATT_Q_HEADS):
        sl = slice(h * HEAD, (h + 1) * HEAD)
        q = _norm_rope_head(q_ref[:, sl].T, qw_ref[...], cos_t, sin_t, ATT_Q_SCALE, axis=0)
        qt_ref[sl, :] = q.astype(qt_ref.dtype)
    cos, sin = cos_ref[...], sin_ref[...]
    for h in range(ATT_KV_HEADS):
        sl = slice(h * HEAD, (h + 1) * HEAD)
        ko_ref[:, sl] = _norm_rope_head(k_ref[:, sl], kw_ref[...], cos, sin, None,
                                        axis=1).astype(ko_ref.dtype)
        vt_ref[0, sl, :] = v_ref[:, sl].T.astype(vt_ref.dtype)


def qk_norm_rope(proj, q_w, k_w, cols):
    T = proj.shape[0]
    tm = ATT_SUB_K
    QW, KW = ATT_Q_HEADS * HEAD, ATT_KV_HEADS * HEAD
    cq, ck, cv = cols
    cos, sin = _rope_tables(T)
    q_w_col = jnp.broadcast_to(q_w.reshape(HEAD, 1), (HEAD, tm))
    return pl.pallas_call(
        _qk_rope_kernel,
        grid=(T // tm,),
        in_specs=[pl.BlockSpec((tm, QW), lambda i: (i, cq // QW)),
                  pl.BlockSpec((tm, KW), lambda i: (i, ck // KW)),
                  pl.BlockSpec((tm, KW), lambda i: (i, cv // KW)),
                  pl.BlockSpec((HEAD, tm), lambda i: (0, 0)),
                  pl.BlockSpec((1, HEAD), lambda i: (0, 0)),
                  pl.BlockSpec((tm, HEAD), lambda i: (i, 0)),
                  pl.BlockSpec((tm, HEAD), lambda i: (i, 0)),
                  pl.BlockSpec((HEAD, tm), lambda i: (0, i)),
                  pl.BlockSpec((HEAD, tm), lambda i: (0, i))],
        out_specs=[pl.BlockSpec((QW, tm), lambda i: (0, i)),
                   pl.BlockSpec((tm, KW), lambda i: (i, 0)),
                   pl.BlockSpec((1, KW, tm), lambda i: (i, 0, 0))],
        out_shape=[jax.ShapeDtypeStruct((QW, T), BF16),
                   jax.ShapeDtypeStruct((T, KW), BF16),
                   jax.ShapeDtypeStruct((T // tm, KW, tm), BF16)],
        compiler_params=_params("parallel"),
        name="qk_norm_rope",
    )(proj, proj, proj, q_w_col, k_w.reshape(1, HEAD), cos, sin, cos.T, sin.T)


def _row_norm_bound(norm_w, scale):
    margin = 1.0 + 2.0 ** -7
    return jnp.sqrt(jnp.float32(HEAD)) * jnp.max(jnp.abs(norm_w)) * (scale * margin)


def _flash_kernel(bounded_ref, qt_ref, k_ref, vt_ref, kmax_ref, o_ref, m_scr, l_scr, acc_scr):
    j = pl.program_id(2)
    group = ATT_Q_HEADS // ATT_KV_HEADS
    n_sub = k_ref.shape[0] // ATT_SUB_K
    bounded = bounded_ref[0] != 0

    def key_block(jj):
        k = k_ref[pl.ds(pl.multiple_of(jj * ATT_SUB_K, ATT_SUB_K), ATT_SUB_K), :]
        return k, vt_ref[jj]

    def logits(k, r):
        return jnp.dot(k, qt_ref[r * HEAD:(r + 1) * HEAD, :], preferred_element_type=F32)

    @pl.when(j == 0)
    def _():
        l_scr[...] = jnp.zeros_like(l_scr)
        acc_scr[...] = jnp.zeros_like(acc_scr)

        @pl.when(bounded)
        def _():
            for r in range(group):
                q = qt_ref[r * HEAD:(r + 1) * HEAD, :].astype(F32)
                m_scr[r] = jnp.sqrt(jnp.sum(q * q, axis=0, keepdims=True)) * kmax_ref[...]

        @pl.when(jnp.logical_not(bounded))
        def _():
            m_scr[...] = jnp.full_like(m_scr, -jnp.inf)

    @pl.when(bounded)
    def _():
        stages = [(jj, r) for jj in range(n_sub) for r in range(group)]

        def stage_logits(stage):
            jj, r = stage
            return logits(k_ref[jj * ATT_SUB_K:(jj + 1) * ATT_SUB_K, :], r)

        st = stage_logits(stages[0])
        for idx, (jj, r) in enumerate(stages):
            st_next = stage_logits(stages[idx + 1]) if idx + 1 < len(stages) else None
            pt = jnp.exp2(st - m_scr[r])
            l_scr[r] += jnp.sum(pt, axis=0, keepdims=True)
            acc_scr[r] += jnp.dot(vt_ref[jj], pt.astype(BF16), preferred_element_type=F32)
            st = st_next

    @pl.when(jnp.logical_not(bounded))
    def _():
        def sub_block(jj, carry):
            k, vt = key_block(jj)
            for r in range(group):
                st = logits(k, r)
                m_prev = m_scr[r]
                m_new = jnp.maximum(m_prev, jnp.max(st, axis=0, keepdims=True))
                alpha = jnp.exp2(m_prev - m_new)
                pt = jnp.exp2(st - m_new)
                l_scr[r] = alpha * l_scr[r] + jnp.sum(pt, axis=0, keepdims=True)
                acc_scr[r] = alpha * acc_scr[r] + jnp.dot(vt, pt.astype(BF16),
                                                          preferred_element_type=F32)
                m_scr[r] = m_new
            return carry

        lax.fori_loop(0, n_sub, sub_block, 0)

    @pl.when(j == pl.num_programs(2) - 1)
    def _():
        for r in range(group):
            o = acc_scr[r] / l_scr[r]
            o_ref[:, r * HEAD:(r + 1) * HEAD] = o.T.astype(o_ref.dtype)


def flash_attention(qt, k, vt, q_max, k_max, tq=512, tk=4096):
    T = k.shape[0]
    group = ATT_Q_HEADS // ATT_KV_HEADS
    GW = group * HEAD
    bounded = (q_max * k_max <= ATT_MAX_BOUNDED_LOGIT).astype(jnp.int32).reshape(1)
    kmax_row = jnp.full((1, tq), k_max, F32)
    grid_spec = pltpu.PrefetchScalarGridSpec(
        num_scalar_prefetch=1,
        grid=(ATT_KV_HEADS, T // tq, T // tk),
        in_specs=[pl.BlockSpec((GW, tq), lambda g, i, j, b: (g, i)),
                  pl.BlockSpec((tk, HEAD), lambda g, i, j, b: (j, g)),
                  pl.BlockSpec((tk // ATT_SUB_K, HEAD, ATT_SUB_K), lambda g, i, j, b: (j, g, 0)),
                  pl.BlockSpec((1, tq), lambda g, i, j, b: (0, 0))],
        out_specs=pl.BlockSpec((tq, GW), lambda g, i, j, b: (i, g)),
        scratch_shapes=[pltpu.VMEM((group, 1, tq), F32),
                        pltpu.VMEM((group, 1, tq), F32),
                        pltpu.VMEM((group, HEAD, tq), F32)])
    return pl.pallas_call(
        _flash_kernel,
        grid_spec=grid_spec,
        out_shape=jax.ShapeDtypeStruct((T, ATT_Q_HEADS * HEAD), BF16),
        compiler_params=_params("parallel", "parallel", "arbitrary"),
        name="flash_attention",
    )(bounded, qt, k, vt, kmax_row)


def _encoder_layer(x, norm_mix_w, w_in, lb_f, lb_b, hg_norm_w, q_norm_w, k_norm_w,
                   w_branch_a, w_branch_b, w_out, w_gate_up, w_down):
    hg_w = HG_HEADS * HEAD
    q_w, kv_w = ATT_Q_HEADS * HEAD, ATT_KV_HEADS * HEAD
    D = x.shape[1]
    sizes = (hg_w, hg_w, hg_w, hg_w, hg_w, q_w, kv_w, kv_w, D, D)
    offs = np.concatenate([[0], np.cumsum(sizes)]).tolist()
    assert offs[-1] == w_in.shape[1]

    h = rmsnorm(x, norm_mix_w, BF16)
    proj = in_projection(h, w_in)
    o_hg = hgrn2_bidirectional(proj, lb_f, lb_b, hg_norm_w, offs[0:5])
    qt, k, vt = qk_norm_rope(proj, q_norm_w, k_norm_w, offs[5:8])
    o_att = flash_attention(qt, k, vt, _row_norm_bound(q_norm_w, ATT_Q_SCALE),
                            _row_norm_bound(k_norm_w, 1.0))
    merged = gated_merge(o_hg, o_att, w_branch_a, w_branch_b, proj, offs[8], offs[9])
    x, xb, sumsq = residual_matmul(merged, w_out, x)
    act = swiglu_up(xb, sumsq, w_gate_up)
    return swiglu_down(act, w_down, x)


def kernel(x_prompt, x_sample, norm_mix_w, w_in, lb_fwd, lb_bwd, hg_norm_w, q_norm_w, k_norm_w,
           w_branch_a, w_branch_b, w_out, norm_ffn_w, w_gate_up, w_down, norm_final_w):
    depth = w_in.shape[0]
    lbs_f = jnp.cumsum(jax.nn.softmax(lb_fwd.astype(F32), axis=0), axis=0)
    lbs_b = jnp.cumsum(jax.nn.softmax(lb_bwd.astype(F32), axis=0), axis=0)
    bf = lambda w: w.astype(BF16)
    w_in, w_branch_a, w_branch_b, w_out, w_down = map(
        bf, (w_in, w_branch_a, w_branch_b, w_out, w_down))
    w_gate_up = bf(norm_ffn_w[:, :, None] * w_gate_up)

    def trunk(x):
        x = x.reshape(x.shape[-2], x.shape[-1])
        for l in range(depth):
            x = _encoder_layer(x, norm_mix_w[l], w_in[l], lbs_f[l], lbs_b[l], hg_norm_w[l],
                               q_norm_w[l], k_norm_w[l], w_branch_a[l], w_branch_b[l], w_out[l],
                               w_gate_up[l], w_down[l])
        return rmsnorm(x, norm_final_w, F32)

    assert x_prompt.shape[0] == 1 and x_sample.shape[0] == 1
    return trunk(x_prompt)[None], trunk(x_sample)[None]
```

```python
import functools

import jax
import jax.numpy as jnp
import numpy as np
from jax import lax
from jax.experimental import pallas as pl
from jax.experimental.pallas import tpu as pltpu

F32 = jnp.float32
BF16 = jnp.bfloat16

EPS = 1e-6
HEAD = 128
HG_HEADS = 16
ATT_Q_HEADS = 16
ATT_KV_HEADS = 4
GRID_W = 64
ROPE_THETA = 10000.0

V7X_VMEM_LIMIT_BYTES = 56 * 1024 * 1024

IN_PROJ_TILE = (1024, 1024)
MERGE_TILE = (1024, 512)
OUT_PROJ_TILE = (1024, 1024)
SWIGLU_UP_TILE = (2048, 256)
SWIGLU_DOWN_TILE = (512, 512)
NORM_ROWS = 256

HG_CHUNK = 64
HG_TILE = 256
HG_GROUP = 16
HG_MAX_EXPONENT = 115.0

ATT_SUB_K = 512
ATT_BLOCK_Q = 512
ATT_BLOCK_K = 8192
ATT_Q_SCALE = float(np.log2(np.e) / np.sqrt(np.float32(HEAD)))
ATT_MAX_BOUNDED_LOGIT = 60.0


def _params(*semantics):
    return pltpu.CompilerParams(dimension_semantics=semantics,
                                vmem_limit_bytes=V7X_VMEM_LIMIT_BYTES)


def _sigmoid(x):
    return 0.5 + 0.5 * jnp.tanh(0.5 * x)


def _silu(x):
    return x * _sigmoid(x)


def _rmsnorm_kernel(x_ref, w_ref, o_ref):
    x = x_ref[...]
    inv = lax.rsqrt(jnp.mean(x * x, axis=-1, keepdims=True) + EPS)
    o_ref[...] = (x * inv * w_ref[...]).astype(o_ref.dtype)


def rmsnorm(x, w, out_dtype):
    T, D = x.shape
    tm = NORM_ROWS
    return pl.pallas_call(
        _rmsnorm_kernel,
        grid=(T // tm,),
        in_specs=[pl.BlockSpec((tm, D), lambda i: (i, 0)),
                  pl.BlockSpec((1, D), lambda i: (0, 0))],
        out_specs=pl.BlockSpec((tm, D), lambda i: (i, 0)),
        out_shape=jax.ShapeDtypeStruct((T, D), out_dtype),
        compiler_params=_params("parallel"),
        name="rmsnorm",
    )(x, w.reshape(1, D))


def _proj_kernel(a_ref, w_ref, o_ref):
    o_ref[...] = jnp.dot(a_ref[...], w_ref[...], preferred_element_type=F32)


def in_projection(h, w):
    tm, tn = IN_PROJ_TILE
    T, K = h.shape
    N = w.shape[1]
    return pl.pallas_call(
        _proj_kernel,
        grid=(T // tm, N // tn),
        in_specs=[pl.BlockSpec((tm, K), lambda i, j: (i, 0)),
                  pl.BlockSpec((K, tn), lambda i, j: (0, j))],
        out_specs=pl.BlockSpec((tm, tn), lambda i, j: (i, j)),
        out_shape=jax.ShapeDtypeStruct((T, N), F32),
        compiler_params=_params("parallel", "arbitrary"),
        name="in_projection",
    )(h, w)


def _merge_kernel(oa_ref, ob_ref, wa_ref, wb_ref, ga_ref, gb_ref, o_ref):
    ya = jnp.dot(oa_ref[...], wa_ref[...], preferred_element_type=F32)
    yb = jnp.dot(ob_ref[...], wb_ref[...], preferred_element_type=F32)
    o_ref[...] = (_sigmoid(ga_ref[...]) * ya + _sigmoid(gb_ref[...]) * yb).astype(o_ref.dtype)


def gated_merge(o_hg, o_att, w_a, w_b, proj, ga_col, gb_col):
    tm, tn = MERGE_TILE
    T, K = o_hg.shape
    D = w_a.shape[1]
    ga_blk, gb_blk = ga_col // tn, gb_col // tn
    return pl.pallas_call(
        _merge_kernel,
        grid=(T // tm, D // tn),
        in_specs=[pl.BlockSpec((tm, K), lambda i, j: (i, 0)),
                  pl.BlockSpec((tm, K), lambda i, j: (i, 0)),
                  pl.BlockSpec((K, tn), lambda i, j: (0, j)),
                  pl.BlockSpec((K, tn), lambda i, j: (0, j)),
                  pl.BlockSpec((tm, tn), lambda i, j: (i, ga_blk + j)),
                  pl.BlockSpec((tm, tn), lambda i, j: (i, gb_blk + j))],
        out_specs=pl.BlockSpec((tm, tn), lambda i, j: (i, j)),
        out_shape=jax.ShapeDtypeStruct((T, D), BF16),
        compiler_params=_params("parallel", "arbitrary"),
        name="gated_merge",
    )(o_hg, o_att, w_a, w_b, proj, proj)


def _residual_mm_kernel(a_ref, w_ref, r_ref, o_ref):
    o_ref[...] = r_ref[...] + jnp.dot(a_ref[...], w_ref[...], preferred_element_type=F32)


def residual_matmul(a, w, res, tile, name):
    tm, tn = tile
    T, K = a.shape
    N = w.shape[1]
    return pl.pallas_call(
        _residual_mm_kernel,
        grid=(T // tm, N // tn),
        in_specs=[pl.BlockSpec((tm, K), lambda i, j: (i, 0)),
                  pl.BlockSpec((K, tn), lambda i, j: (0, j)),
                  pl.BlockSpec((tm, tn), lambda i, j: (i, j))],
        out_specs=pl.BlockSpec((tm, tn), lambda i, j: (i, j)),
        out_shape=jax.ShapeDtypeStruct((T, N), F32),
        compiler_params=_params("parallel", "arbitrary"),
        name=name,
    )(a, w, res)


def _swiglu_up_kernel(a_ref, wg_ref, wu_ref, o_ref):
    a = a_ref[...]
    g = jnp.dot(a, wg_ref[...], preferred_element_type=F32)
    u = jnp.dot(a, wu_ref[...], preferred_element_type=F32)
    o_ref[...] = (_silu(g) * u).astype(o_ref.dtype)


def swiglu_up(h, w_gate_up):
    tm, tn = SWIGLU_UP_TILE
    T, K = h.shape
    F = w_gate_up.shape[1] // 2
    u_blk = F // tn
    return pl.pallas_call(
        _swiglu_up_kernel,
        grid=(T // tm, F // tn),
        in_specs=[pl.BlockSpec((tm, K), lambda i, j: (i, 0)),
                  pl.BlockSpec((K, tn), lambda i, j: (0, j)),
                  pl.BlockSpec((K, tn), lambda i, j: (0, u_blk + j))],
        out_specs=pl.BlockSpec((tm, tn), lambda i, j: (i, j)),
        out_shape=jax.ShapeDtypeStruct((T, F), BF16),
        compiler_params=_params("parallel", "arbitrary"),
        name="swiglu_up",
    )(h, w_gate_up, w_gate_up)


def _split2_bf16(x):
    hi = x.astype(BF16)
    return hi, (x - hi.astype(F32)).astype(BF16)


def _forget_gate(z, lb):
    half = 0.5 * (1.0 - lb)
    kk = half - half * jnp.tanh(0.5 * z)
    f = 1.0 - kk
    return f, jnp.log2(f), kk


def _hgrn_scan_tile(q_ref, z_ref, v_ref, lb_ref, st_ref, st_save, o_scr, vt_scr, emit, reverse):
    C = HG_CHUNK
    n_chunks = HG_TILE // C
    chunks = [slice(j * C, (j + 1) * C) for j in range(n_chunks)]
    row = lax.broadcasted_iota(jnp.int32, (HG_TILE, HG_TILE), 0)
    col = lax.broadcasted_iota(jnp.int32, (HG_TILE, HG_TILE), 1)
    visible = (col >= row) if reverse else (col <= row)
    chunk_bits = C.bit_length() - 1
    assert C == 1 << chunk_bits
    same_chunk = lax.shift_right_logical(row, chunk_bits) == lax.shift_right_logical(col, chunk_bits)
    keep = visible & same_chunk
    tri = jnp.where(keep, 1.0, 0.0).astype(BF16)
    end = 0 if reverse else C - 1
    mid = C // 2 if reverse else C // 2 - 1
    heads = [slice(g * HEAD, (g + 1) * HEAD) for g in range(HG_GROUP)]
    nt_dims = (((1,), (1,)), ((), ()))

    def rel(c, off):
        return jnp.concatenate([c[sl] - c[sl.start + off:sl.start + off + 1] for sl in chunks],
                               axis=0)

    st_save[...] = st_ref[...]

    keys, cums = [], []
    for hs in heads:
        _, logf, kk = _forget_gate(z_ref[:, hs], lb_ref[:, hs])
        keys.append(kk)
        cs = jnp.dot(tri, jnp.concatenate(_split2_bf16(logf), axis=1),
                     preferred_element_type=F32)
        cums.append(cs[:, :HEAD] + cs[:, HEAD:])
    mids = [rel(c, mid) for c in cums]
    worst = functools.reduce(jnp.maximum, [jnp.max(jnp.abs(a)) for a in mids])
    scores = []
    for g, hs in enumerate(heads):
        q_in = (q_ref[:, hs] * jnp.exp2(mids[g])).astype(BF16)
        k_in = (keys[g] * jnp.exp2(-mids[g])).astype(BF16)
        scores.append(lax.dot_general(q_in, k_in, nt_dims, preferred_element_type=F32))
    q_state, decay, update = [], {}, {}
    for g, hs in enumerate(heads):
        c, v = cums[g], v_ref[:, hs]
        k_st = (keys[g] * jnp.exp2(-rel(c, end))).astype(BF16)
        q_state.append((q_ref[:, hs] * jnp.exp2(c)).astype(BF16))
        for j, sl in enumerate(chunks):
            update[g, j] = jnp.dot(v[sl].T.astype(BF16), k_st[sl], preferred_element_type=F32)
            decay[g, j] = jnp.exp2(c[sl.start + end:sl.start + end + 1])
    for g, hs in enumerate(heads):
        s = jnp.where(keep, scores[g], 0.0).astype(BF16)
        o_scr[:, hs] = jnp.dot(s, v_ref[:, hs].astype(BF16), preferred_element_type=F32)
    order = range(n_chunks - 1, -1, -1) if reverse else range(n_chunks)
    for j in order:
        sl = chunks[j]
        for g, hs in enumerate(heads):
            st = st_ref[g]
            o_scr[sl, hs] += lax.dot_general(q_state[g][sl], st.astype(BF16), nt_dims,
                                             preferred_element_type=F32)
            st_ref[g] = st * decay[g, j] + update[g, j]
    emit()

    @pl.when(jnp.logical_not(worst < HG_MAX_EXPONENT))
    def _():
        st_ref[...] = st_save[...]
        lane = lax.broadcasted_iota(jnp.int32, (1, HG_TILE), 1)
        rows = lax.broadcasted_iota(jnp.int32, (HG_TILE, 1), 0)
        for g, hs in enumerate(heads):
            z, lb = z_ref[:, hs], lb_ref[:, hs]
            q = q_ref[:, hs]
            vt_scr[...] = v_ref[:, hs].T

            def body(i, carry):
                t = (HG_TILE - 1 - i) if reverse else i
                onehot = (lane == t).astype(F32)
                pick = lambda x: jnp.sum(jnp.where(rows == t, x, 0.0), axis=0, keepdims=True)
                f_t, _, k_t = _forget_gate(pick(z), lb)
                q_t = pick(q)
                v_col = jnp.sum(vt_scr[...] * onehot, axis=1, keepdims=True)
                st = st_ref[g] * f_t + v_col * k_t
                st_ref[g] = st
                o_col = jnp.sum(st * q_t, axis=1, keepdims=True)
                return carry + o_col * onehot

            ot = lax.fori_loop(0, HG_TILE, body, jnp.zeros((HEAD, HG_TILE), F32))
            o_scr[:, hs] = ot.T
        emit()


def _hgrn_fwd_kernel(q_ref, z_ref, v_ref, lb_ref, o_ref, st_ref, st_save, o_scr, vt_scr):
    @pl.when(pl.program_id(1) == 0)
    def _():
        st_ref[...] = jnp.zeros_like(st_ref)

    def emit():
        o_ref[...] = o_scr[...]

    _hgrn_scan_tile(q_ref, z_ref, v_ref, lb_ref, st_ref, st_save, o_scr, vt_scr, emit,
                    reverse=False)


def _hgrn_bwd_kernel(q_ref, z_ref, v_ref, lb_ref, of_ref, gate_ref, nw_ref, o_ref,
                     st_ref, st_save, o_scr, vt_scr):
    @pl.when(pl.program_id(1) == 0)
    def _():
        st_ref[...] = jnp.zeros_like(st_ref)

    def emit():
        for g in range(HG_GROUP):
            hs = slice(g * HEAD, (g + 1) * HEAD)
            o = of_ref[:, hs] + o_scr[:, hs]
            o = o * lax.rsqrt(jnp.mean(o * o, axis=-1, keepdims=True) + EPS) * nw_ref[:, hs]
            o_ref[:, hs] = (o * _silu(gate_ref[:, hs])).astype(o_ref.dtype)

    _hgrn_scan_tile(q_ref, z_ref, v_ref, lb_ref, st_ref, st_save, o_scr, vt_scr, emit,
                    reverse=True)


def hgrn2_bidirectional(proj, lb_f, lb_b, norm_w, cols):
    T = proj.shape[0]
    n_tiles = T // HG_TILE
    W = HG_HEADS * HEAD
    GW = HG_GROUP * HEAD
    n_groups = HG_HEADS // HG_GROUP
    cq, czf, czb, ci, cg = (c // GW for c in cols)
    scratch = [pltpu.VMEM((HG_GROUP, HEAD, HEAD), F32),
               pltpu.VMEM((HG_GROUP, HEAD, HEAD), F32),
               pltpu.VMEM((HG_TILE, GW), F32),
               pltpu.VMEM((HEAD, HG_TILE), F32)]
    tile = lambda blk: pl.BlockSpec((HG_TILE, GW), lambda h, i: (i, blk + h))
    vec = pl.BlockSpec((1, GW), lambda h, i: (0, h))
    o_fwd = pl.pallas_call(
        _hgrn_fwd_kernel,
        grid=(n_groups, n_tiles),
        in_specs=[tile(cq), tile(czf), tile(ci), vec],
        out_specs=pl.BlockSpec((HG_TILE, GW), lambda h, i: (i, h)),
        out_shape=jax.ShapeDtypeStruct((T, W), F32),
        scratch_shapes=scratch,
        compiler_params=_params("parallel", "arbitrary"),
        name="hgrn_fwd",
    )(proj, proj, proj, lb_f.reshape(1, W))

    last = n_tiles - 1
    rtile = lambda blk: pl.BlockSpec((HG_TILE, GW), lambda h, i: (last - i, blk + h))
    return pl.pallas_call(
        _hgrn_bwd_kernel,
        grid=(n_groups, n_tiles),
        in_specs=[rtile(cq), rtile(czb), rtile(ci), vec, rtile(0), rtile(cg), vec],
        out_specs=pl.BlockSpec((HG_TILE, GW), lambda h, i: (last - i, h)),
        out_shape=jax.ShapeDtypeStruct((T, W), BF16),
        scratch_shapes=scratch,
        compiler_params=_params("parallel", "arbitrary"),
        name="hgrn_bwd",
    )(proj, proj, proj, lb_b.reshape(1, W), o_fwd, proj, norm_w.reshape(1, W))


def _rope_tables(T):
    rows = T // GRID_W
    row = jnp.repeat(jnp.arange(rows, dtype=F32), GRID_W)
    col = jnp.tile(jnp.arange(GRID_W, dtype=F32), rows)
    axis_dim = HEAD // 2
    inv = ROPE_THETA ** (-jnp.arange(0, axis_dim, 2, dtype=F32) / axis_dim)
    ang = jnp.concatenate([row[:, None] * inv, col[:, None] * inv], axis=-1)
    cos = jnp.repeat(jnp.cos(ang), 2, axis=-1)
    sin = jnp.repeat(jnp.sin(ang), 2, axis=-1)
    sign = jnp.tile(jnp.array([-1.0, 1.0], F32), HEAD // 2)
    return cos, sin * sign


def _norm_rope_head(x, w, cos, sin_signed, scale, axis):
    x = x * lax.rsqrt(jnp.mean(x * x, axis=axis, keepdims=True) + EPS) * w
    pos = lax.broadcasted_iota(jnp.int32, x.shape, axis)
    partner = jnp.where(pos % 2 == 0,
                        pltpu.roll(x, HEAD - 1, axis=axis),
                        pltpu.roll(x, 1, axis=axis))
    out = x * cos + partner * sin_signed
    return out * scale if scale is not None else out


def _qk_rope_kernel(q_ref, k_ref, v_ref, qw_ref, kw_ref, cos_ref, sin_ref, cos_t_ref, sin_t_ref,
                    qt_ref, ko_ref, vt_ref):
    cos_t, sin_t = cos_t_ref[...], sin_t_ref[...]
    for h in range(ATT_Q_HEADS):
        sl = slice(h * HEAD, (h + 1) * HEAD)
        q = _norm_rope_head(q_ref[:, sl].T, qw_ref[...], cos_t, sin_t, ATT_Q_SCALE, axis=0)
        qt_ref[sl, :] = q.astype(qt_ref.dtype)
    cos, sin = cos_ref[...], sin_ref[...]
    for h in range(ATT_KV_HEADS):
        sl = slice(h * HEAD, (h + 1) * HEAD)
        ko_ref[:, sl] = _norm_rope_head(k_ref[:, sl], kw_ref[...], cos, sin, None,
                                        axis=1).astype(ko_ref.dtype)
        vt_ref[0, sl, :] = v_ref[:, sl].T.astype(vt_ref.dtype)


def qk_norm_rope(proj, q_w, k_w, cols):
    T = proj.shape[0]
    tm = ATT_SUB_K
    QW, KW = ATT_Q_HEADS * HEAD, ATT_KV_HEADS * HEAD
    cq, ck, cv = cols
    cos, sin = _rope_tables(T)
    q_w_col = jnp.broadcast_to(q_w.reshape(HEAD, 1), (HEAD, tm))
    return pl.pallas_call(
        _qk_rope_kernel,
        grid=(T // tm,),
        in_specs=[pl.BlockSpec((tm, QW), lambda i: (i, cq // QW)),
                  pl.BlockSpec((tm, KW), lambda i: (i, ck // KW)),
                  pl.BlockSpec((tm, KW), lambda i: (i, cv // KW)),
                  pl.BlockSpec((HEAD, tm), lambda i: (0, 0)),
                  pl.BlockSpec((1, HEAD), lambda i: (0, 0)),
                  pl.BlockSpec((tm, HEAD), lambda i: (i, 0)),
                  pl.BlockSpec((tm, HEAD), lambda i: (i, 0)),
                  pl.BlockSpec((HEAD, tm), lambda i: (0, i)),
                  pl.BlockSpec((HEAD, tm), lambda i: (0, i))],
        out_specs=[pl.BlockSpec((QW, tm), lambda i: (0, i)),
                   pl.BlockSpec((tm, KW), lambda i: (i, 0)),
                   pl.BlockSpec((1, KW, tm), lambda i: (i, 0, 0))],
        out_shape=[jax.ShapeDtypeStruct((QW, T), BF16),
                   jax.ShapeDtypeStruct((T, KW), BF16),
                   jax.ShapeDtypeStruct((T // tm, KW, tm), BF16)],
        compiler_params=_params("parallel"),
        name="qk_norm_rope",
    )(proj, proj, proj, q_w_col, k_w.reshape(1, HEAD), cos, sin, cos.T, sin.T)


def _row_norm_bound(norm_w, scale):
    margin = 1.0 + 2.0 ** -7
    return jnp.sqrt(jnp.float32(HEAD)) * jnp.max(jnp.abs(norm_w)) * (scale * margin)


def _flash_kernel(bounded_ref, qt_ref, k_ref, vt_ref, kmax_ref, o_ref, m_scr, l_scr, acc_scr):
    j = pl.program_id(2)
    group = ATT_Q_HEADS // ATT_KV_HEADS
    n_sub = k_ref.shape[0] // ATT_SUB_K
    bounded = bounded_ref[0] != 0

    def key_block(jj):
        k = k_ref[pl.ds(pl.multiple_of(jj * ATT_SUB_K, ATT_SUB_K), ATT_SUB_K), :]
        return k, vt_ref[jj]

    def logits(k, r):
        return jnp.dot(k, qt_ref[r * HEAD:(r + 1) * HEAD, :], preferred_element_type=F32)

    @pl.when(j == 0)
    def _():
        l_scr[...] = jnp.zeros_like(l_scr)
        acc_scr[...] = jnp.zeros_like(acc_scr)

        @pl.when(bounded)
        def _():
            for r in range(group):
                q = qt_ref[r * HEAD:(r + 1) * HEAD, :].astype(F32)
                m_scr[r] = jnp.sqrt(jnp.sum(q * q, axis=0, keepdims=True)) * kmax_ref[...]

        @pl.when(jnp.logical_not(bounded))
        def _():
            m_scr[...] = jnp.full_like(m_scr, -jnp.inf)

    @pl.when(bounded)
    def _():
        stages = [(jj, r) for jj in range(n_sub) for r in range(group)]

        def stage_logits(stage):
            jj, r = stage
            return logits(k_ref[jj * ATT_SUB_K:(jj + 1) * ATT_SUB_K, :], r)

        st = stage_logits(stages[0])
        for idx, (jj, r) in enumerate(stages):
            st_next = stage_logits(stages[idx + 1]) if idx + 1 < len(stages) else None
            pt = jnp.exp2(st - m_scr[r])
            l_scr[r] += jnp.sum(pt, axis=0, keepdims=True)
            acc_scr[r] += jnp.dot(vt_ref[jj], pt.astype(BF16), preferred_element_type=F32)
            st = st_next

    @pl.when(jnp.logical_not(bounded))
    def _():
        def sub_block(jj, carry):
            k, vt = key_block(jj)
            for r in range(group):
                st = logits(k, r)
                m_prev = m_scr[r]
                m_new = jnp.maximum(m_prev, jnp.max(st, axis=0, keepdims=True))
                alpha = jnp.exp2(m_prev - m_new)
                pt = jnp.exp2(st - m_new)
                l_scr[r] = alpha * l_scr[r] + jnp.sum(pt, axis=0, keepdims=True)
                acc_scr[r] = alpha * acc_scr[r] + jnp.dot(vt, pt.astype(BF16),
                                                          preferred_element_type=F32)
                m_scr[r] = m_new
            return carry

        lax.fori_loop(0, n_sub, sub_block, 0)

    @pl.when(j == pl.num_programs(2) - 1)
    def _():
        for r in range(group):
            o = acc_scr[r] / l_scr[r]
            o_ref[:, r * HEAD:(r + 1) * HEAD] = o.T.astype(o_ref.dtype)


def flash_attention(qt, k, vt, q_max, k_max, tq=ATT_BLOCK_Q, tk=ATT_BLOCK_K):
    T = k.shape[0]
    tk = min(tk, T)
    group = ATT_Q_HEADS // ATT_KV_HEADS
    GW = group * HEAD
    bounded = (q_max * k_max <= ATT_MAX_BOUNDED_LOGIT).astype(jnp.int32).reshape(1)
    kmax_row = jnp.full((1, tq), k_max, F32)
    grid_spec = pltpu.PrefetchScalarGridSpec(
        num_scalar_prefetch=1,
        grid=(ATT_KV_HEADS, T // tq, T // tk),
        in_specs=[pl.BlockSpec((GW, tq), lambda g, i, j, b: (g, i)),
                  pl.BlockSpec((tk, HEAD), lambda g, i, j, b: (j, g)),
                  pl.BlockSpec((tk // ATT_SUB_K, HEAD, ATT_SUB_K), lambda g, i, j, b: (j, g, 0)),
                  pl.BlockSpec((1, tq), lambda g, i, j, b: (0, 0))],
        out_specs=pl.BlockSpec((tq, GW), lambda g, i, j, b: (i, g)),
        scratch_shapes=[pltpu.VMEM((group, 1, tq), F32),
                        pltpu.VMEM((group, 1, tq), F32),
                        pltpu.VMEM((group, HEAD, tq), F32)])
    return pl.pallas_call(
        _flash_kernel,
        grid_spec=grid_spec,
        out_shape=jax.ShapeDtypeStruct((T, ATT_Q_HEADS * HEAD), BF16),
        compiler_params=_params("parallel", "parallel", "arbitrary"),
        name="flash_attention",
    )(bounded, qt, k, vt, kmax_row)


def _encoder_layer(x, norm_mix_w, w_in, lb_f, lb_b, hg_norm_w, q_norm_w, k_norm_w,
                   w_branch_a, w_branch_b, w_out, norm_ffn_w, w_gate_up, w_down):
    hg_w = HG_HEADS * HEAD
    q_w, kv_w = ATT_Q_HEADS * HEAD, ATT_KV_HEADS * HEAD
    D = x.shape[1]
    sizes = (hg_w, hg_w, hg_w, hg_w, hg_w, q_w, kv_w, kv_w, D, D)
    offs = np.concatenate([[0], np.cumsum(sizes)]).tolist()
    assert offs[-1] == w_in.shape[1]

    h = rmsnorm(x, norm_mix_w, BF16)
    proj = in_projection(h, w_in)
    o_hg = hgrn2_bidirectional(proj, lb_f, lb_b, hg_norm_w, offs[0:5])
    qt, k, vt = qk_norm_rope(proj, q_norm_w, k_norm_w, offs[5:8])
    o_att = flash_attention(qt, k, vt, _row_norm_bound(q_norm_w, ATT_Q_SCALE),
                            _row_norm_bound(k_norm_w, 1.0))
    merged = gated_merge(o_hg, o_att, w_branch_a, w_branch_b, proj, offs[8], offs[9])
    x = residual_matmul(merged, w_out, x, OUT_PROJ_TILE, "out_projection")
    h = rmsnorm(x, norm_ffn_w, BF16)
    act = swiglu_up(h, w_gate_up)
    return residual_matmul(act, w_down, x, SWIGLU_DOWN_TILE, "swiglu_down")


def kernel(x_prompt, x_sample, norm_mix_w, w_in, lb_fwd, lb_bwd, hg_norm_w, q_norm_w, k_norm_w,
           w_branch_a, w_branch_b, w_out, norm_ffn_w, w_gate_up, w_down, norm_final_w):
    depth = w_in.shape[0]
    lbs_f = jnp.cumsum(jax.nn.softmax(lb_fwd.astype(F32), axis=0), axis=0)
    lbs_b = jnp.cumsum(jax.nn.softmax(lb_bwd.astype(F32), axis=0), axis=0)
    bf = lambda w: w.astype(BF16)
    w_in, w_branch_a, w_branch_b, w_out, w_gate_up, w_down = map(
        bf, (w_in, w_branch_a, w_branch_b, w_out, w_gate_up, w_down))

    def trunk(x):
        x = x.reshape(x.shape[-2], x.shape[-1])
        for l in range(depth):
            x = _encoder_layer(x, norm_mix_w[l], w_in[l], lbs_f[l], lbs_b[l], hg_norm_w[l],
                               q_norm_w[l], k_norm_w[l], w_branch_a[l], w_branch_b[l], w_out[l],
                               norm_ffn_w[l], w_gate_up[l], w_down[l])
        return rmsnorm(x, norm_final_w, F32)

    assert x_prompt.shape[0] == 1 and x_sample.shape[0] == 1
    return trunk(x_prompt)[None], trunk(x_sample)[None]
```

```python
import functools

import jax
import jax.numpy as jnp
import numpy as np
from jax import lax
from jax.experimental import pallas as pl
from jax.experimental.pallas import tpu as pltpu

F32 = jnp.float32
BF16 = jnp.bfloat16

EPS = 1e-6
HEAD = 128
HG_HEADS = 16
ATT_Q_HEADS = 16
ATT_KV_HEADS = 4
GRID_W = 64
ROPE_THETA = 10000.0

V7X_VMEM_LIMIT_BYTES = 56 * 1024 * 1024

IN_PROJ_TILE = (1024, 1024)
MERGE_TILE = (1024, 512)
OUT_PROJ_TILE = (1024, 1024)
SWIGLU_UP_TILE = (2048, 256)
SWIGLU_DOWN_TILE = (512, 512)
NORM_ROWS = 512

HG_CHUNK = 64
HG_TILE = 256
HG_GROUP = 16
HG_MAX_EXPONENT = 115.0

ATT_SUB_K = 512
ATT_BLOCK_Q = 512
ATT_BLOCK_K = 8192
ATT_Q_SCALE = float(np.log2(np.e) / np.sqrt(np.float32(HEAD)))
ATT_MAX_BOUNDED_LOGIT = 60.0


def _params(*semantics):
    return pltpu.CompilerParams(dimension_semantics=semantics,
                                vmem_limit_bytes=V7X_VMEM_LIMIT_BYTES)


def _sigmoid(x):
    return 0.5 + 0.5 * jnp.tanh(0.5 * x)


def _silu(x):
    return x * _sigmoid(x)


def _rmsnorm_kernel(x_ref, w_ref, o_ref):
    x = x_ref[...]
    inv = lax.rsqrt(jnp.mean(x * x, axis=-1, keepdims=True) + EPS)
    o_ref[...] = (x * inv * w_ref[...]).astype(o_ref.dtype)


def rmsnorm(x, w, out_dtype):
    T, D = x.shape
    tm = NORM_ROWS
    assert T % tm == 0
    return pl.pallas_call(
        _rmsnorm_kernel,
        grid=(T // tm,),
        in_specs=[pl.BlockSpec((tm, D), lambda i: (i, 0)),
                  pl.BlockSpec((1, D), lambda i: (0, 0))],
        out_specs=pl.BlockSpec((tm, D), lambda i: (i, 0)),
        out_shape=jax.ShapeDtypeStruct((T, D), out_dtype),
        compiler_params=_params("parallel"),
        name="rmsnorm",
    )(x, w.reshape(1, D))


def _proj_kernel(a_ref, w_ref, o_ref):
    o_ref[...] = jnp.dot(a_ref[...], w_ref[...], preferred_element_type=F32)


def in_projection(h, w):
    tm, tn = IN_PROJ_TILE
    T, K = h.shape
    N = w.shape[1]
    assert T % tm == 0 and N % tn == 0
    return pl.pallas_call(
        _proj_kernel,
        grid=(T // tm, N // tn),
        in_specs=[pl.BlockSpec((tm, K), lambda i, j: (i, 0)),
                  pl.BlockSpec((K, tn), lambda i, j: (0, j))],
        out_specs=pl.BlockSpec((tm, tn), lambda i, j: (i, j)),
        out_shape=jax.ShapeDtypeStruct((T, N), F32),
        compiler_params=_params("parallel", "arbitrary"),
        name="in_projection",
    )(h, w)


def _merge_kernel(oa_ref, ob_ref, wa_ref, wb_ref, ga_ref, gb_ref, o_ref):
    ya = jnp.dot(oa_ref[...], wa_ref[...], preferred_element_type=F32)
    yb = jnp.dot(ob_ref[...], wb_ref[...], preferred_element_type=F32)
    o_ref[...] = (_sigmoid(ga_ref[...]) * ya + _sigmoid(gb_ref[...]) * yb).astype(o_ref.dtype)


def gated_merge(o_hg, o_att, w_a, w_b, proj, ga_col, gb_col):
    tm, tn = MERGE_TILE
    T, K = o_hg.shape
    D = w_a.shape[1]
    assert T % tm == 0 and D % tn == 0 and ga_col % tn == 0 and gb_col % tn == 0
    ga_blk, gb_blk = ga_col // tn, gb_col // tn
    return pl.pallas_call(
        _merge_kernel,
        grid=(T // tm, D // tn),
        in_specs=[pl.BlockSpec((tm, K), lambda i, j: (i, 0)),
                  pl.BlockSpec((tm, K), lambda i, j: (i, 0)),
                  pl.BlockSpec((K, tn), lambda i, j: (0, j)),
                  pl.BlockSpec((K, tn), lambda i, j: (0, j)),
                  pl.BlockSpec((tm, tn), lambda i, j: (i, ga_blk + j)),
                  pl.BlockSpec((tm, tn), lambda i, j: (i, gb_blk + j))],
        out_specs=pl.BlockSpec((tm, tn), lambda i, j: (i, j)),
        out_shape=jax.ShapeDtypeStruct((T, D), BF16),
        compiler_params=_params("parallel", "arbitrary"),
        name="gated_merge",
    )(o_hg, o_att, w_a, w_b, proj, proj)


def _residual_mm_kernel(a_ref, w_ref, r_ref, o_ref):
    o_ref[...] = r_ref[...] + jnp.dot(a_ref[...], w_ref[...], preferred_element_type=F32)


def residual_matmul(a, w, res, tile, name):
    tm, tn = tile
    T, K = a.shape
    N = w.shape[1]
    assert T % tm == 0 and N % tn == 0
    return pl.pallas_call(
        _residual_mm_kernel,
        grid=(T // tm, N // tn),
        in_specs=[pl.BlockSpec((tm, K), lambda i, j: (i, 0)),
                  pl.BlockSpec((K, tn), lambda i, j: (0, j)),
                  pl.BlockSpec((tm, tn), lambda i, j: (i, j))],
        out_specs=pl.BlockSpec((tm, tn), lambda i, j: (i, j)),
        out_shape=jax.ShapeDtypeStruct((T, N), F32),
        compiler_params=_params("parallel", "arbitrary"),
        name=name,
    )(a, w, res)


def _swiglu_up_kernel(a_ref, wg_ref, wu_ref, o_ref):
    a = a_ref[...]
    g = jnp.dot(a, wg_ref[...], preferred_element_type=F32)
    u = jnp.dot(a, wu_ref[...], preferred_element_type=F32)
    o_ref[...] = (_silu(g) * u).astype(o_ref.dtype)


def swiglu_up(h, w_gate_up):
    tm, tn = SWIGLU_UP_TILE
    T, K = h.shape
    F = w_gate_up.shape[1] // 2
    assert T % tm == 0 and F % tn == 0
    u_blk = F // tn
    return pl.pallas_call(
        _swiglu_up_kernel,
        grid=(T // tm, F // tn),
        in_specs=[pl.BlockSpec((tm, K), lambda i, j: (i, 0)),
                  pl.BlockSpec((K, tn), lambda i, j: (0, j)),
                  pl.BlockSpec((K, tn), lambda i, j: (0, u_blk + j))],
        out_specs=pl.BlockSpec((tm, tn), lambda i, j: (i, j)),
        out_shape=jax.ShapeDtypeStruct((T, F), BF16),
        compiler_params=_params("parallel", "arbitrary"),
        name="swiglu_up",
    )(h, w_gate_up, w_gate_up)


def _split2_bf16(x):
    hi = x.astype(BF16)
    return hi, (x - hi.astype(F32)).astype(BF16)


def _forget_gate(z, lb):
    half = 0.5 * (1.0 - lb)
    kk = half - half * jnp.tanh(0.5 * z)
    f = 1.0 - kk
    return f, jnp.log2(f), kk


def _hgrn_scan_tile(q_ref, z_ref, v_ref, lb_ref, st_ref, st_save, o_scr, vt_scr, emit, reverse):
    C = HG_CHUNK
    n_chunks = HG_TILE // C
    chunks = [slice(j * C, (j + 1) * C) for j in range(n_chunks)]
    row = lax.broadcasted_iota(jnp.int32, (HG_TILE, HG_TILE), 0)
    col = lax.broadcasted_iota(jnp.int32, (HG_TILE, HG_TILE), 1)
    visible = (col >= row) if reverse else (col <= row)
    chunk_bits = C.bit_length() - 1
    assert C == 1 << chunk_bits
    same_chunk = lax.shift_right_logical(row, chunk_bits) == lax.shift_right_logical(col, chunk_bits)
    keep = visible & same_chunk
    tri = jnp.where(keep, 1.0, 0.0).astype(BF16)
    end = 0 if reverse else C - 1
    mid = C // 2 if reverse else C // 2 - 1
    heads = [slice(g * HEAD, (g + 1) * HEAD) for g in range(HG_GROUP)]
    nt_dims = (((1,), (1,)), ((), ()))

    def rel(c, off):
        return jnp.concatenate([c[sl] - c[sl.start + off:sl.start + off + 1] for sl in chunks],
                               axis=0)

    st_save[...] = st_ref[...]

    keys, cums = [], []
    for hs in heads:
        _, logf, kk = _forget_gate(z_ref[:, hs], lb_ref[:, hs])
        keys.append(kk)
        cs = jnp.dot(tri, jnp.concatenate(_split2_bf16(logf), axis=1),
                     preferred_element_type=F32)
        cums.append(cs[:, :HEAD] + cs[:, HEAD:])
    mids = [rel(c, mid) for c in cums]
    worst = functools.reduce(jnp.maximum, [jnp.max(jnp.abs(a)) for a in mids])
    scores = []
    for g, hs in enumerate(heads):
        q_in = (q_ref[:, hs] * jnp.exp2(mids[g])).astype(BF16)
        k_in = (keys[g] * jnp.exp2(-mids[g])).astype(BF16)
        scores.append(lax.dot_general(q_in, k_in, nt_dims, preferred_element_type=F32))
    q_state, decay, update = [], {}, {}
    for g, hs in enumerate(heads):
        c, v = cums[g], v_ref[:, hs]
        k_st = (keys[g] * jnp.exp2(-rel(c, end))).astype(BF16)
        q_state.append((q_ref[:, hs] * jnp.exp2(c)).astype(BF16))
        for j, sl in enumerate(chunks):
            update[g, j] = jnp.dot(v[sl].T.astype(BF16), k_st[sl], preferred_element_type=F32)
            decay[g, j] = jnp.exp2(c[sl.start + end:sl.start + end + 1])
    for g, hs in enumerate(heads):
        s = jnp.where(keep, scores[g], 0.0).astype(BF16)
        o_scr[:, hs] = jnp.dot(s, v_ref[:, hs].astype(BF16), preferred_element_type=F32)
    order = range(n_chunks - 1, -1, -1) if reverse else range(n_chunks)
    for j in order:
        sl = chunks[j]
        for g, hs in enumerate(heads):
            st = st_ref[g]
            o_scr[sl, hs] += lax.dot_general(q_state[g][sl], st.astype(BF16), nt_dims,
                                             preferred_element_type=F32)
            st_ref[g] = st * decay[g, j] + update[g, j]
    emit()

    @pl.when(jnp.logical_not(worst < HG_MAX_EXPONENT))
    def _():
        st_ref[...] = st_save[...]
        lane = lax.broadcasted_iota(jnp.int32, (1, HG_TILE), 1)
        rows = lax.broadcasted_iota(jnp.int32, (HG_TILE, 1), 0)
        for g, hs in enumerate(heads):
            z, lb = z_ref[:, hs], lb_ref[:, hs]
            q = q_ref[:, hs]
            vt_scr[...] = v_ref[:, hs].T

            def body(i, carry):
                t = (HG_TILE - 1 - i) if reverse else i
                onehot = (lane == t).astype(F32)
                pick = lambda x: jnp.sum(jnp.where(rows == t, x, 0.0), axis=0, keepdims=True)
                f_t, _, k_t = _forget_gate(pick(z), lb)
                q_t = pick(q)
                v_col = jnp.sum(vt_scr[...] * onehot, axis=1, keepdims=True)
                st = st_ref[g] * f_t + v_col * k_t
                st_ref[g] = st
                o_col = jnp.sum(st * q_t, axis=1, keepdims=True)
                return carry + o_col * onehot

            ot = lax.fori_loop(0, HG_TILE, body, jnp.zeros((HEAD, HG_TILE), F32))
            o_scr[:, hs] = ot.T
        emit()


def _hgrn_fwd_kernel(q_ref, z_ref, v_ref, lb_ref, o_ref, st_ref, st_save, o_scr, vt_scr):
    @pl.when(pl.program_id(1) == 0)
    def _():
        st_ref[...] = jnp.zeros_like(st_ref)

    def emit():
        o_ref[...] = o_scr[...]

    _hgrn_scan_tile(q_ref, z_ref, v_ref, lb_ref, st_ref, st_save, o_scr, vt_scr, emit,
                    reverse=False)


def _hgrn_bwd_kernel(q_ref, z_ref, v_ref, lb_ref, of_ref, gate_ref, nw_ref, o_ref,
                     st_ref, st_save, o_scr, vt_scr):
    @pl.when(pl.program_id(1) == 0)
    def _():
        st_ref[...] = jnp.zeros_like(st_ref)

    def emit():
        for g in range(HG_GROUP):
            hs = slice(g * HEAD, (g + 1) * HEAD)
            o = of_ref[:, hs] + o_scr[:, hs]
            o = o * lax.rsqrt(jnp.mean(o * o, axis=-1, keepdims=True) + EPS) * nw_ref[:, hs]
            o_ref[:, hs] = (o * _silu(gate_ref[:, hs])).astype(o_ref.dtype)

    _hgrn_scan_tile(q_ref, z_ref, v_ref, lb_ref, st_ref, st_save, o_scr, vt_scr, emit,
                    reverse=True)


def hgrn2_bidirectional(proj, lb_f, lb_b, norm_w, cols):
    T = proj.shape[0]
    assert T % HG_TILE == 0 and HG_HEADS % HG_GROUP == 0
    n_tiles = T // HG_TILE
    W = HG_HEADS * HEAD
    GW = HG_GROUP * HEAD
    n_groups = HG_HEADS // HG_GROUP
    cq, czf, czb, ci, cg = (c // GW for c in cols)
    scratch = [pltpu.VMEM((HG_GROUP, HEAD, HEAD), F32),
               pltpu.VMEM((HG_GROUP, HEAD, HEAD), F32),
               pltpu.VMEM((HG_TILE, GW), F32),
               pltpu.VMEM((HEAD, HG_TILE), F32)]
    tile = lambda blk: pl.BlockSpec((HG_TILE, GW), lambda h, i: (i, blk + h))
    vec = pl.BlockSpec((1, GW), lambda h, i: (0, h))
    o_fwd = pl.pallas_call(
        _hgrn_fwd_kernel,
        grid=(n_groups, n_tiles),
        in_specs=[tile(cq), tile(czf), tile(ci), vec],
        out_specs=pl.BlockSpec((HG_TILE, GW), lambda h, i: (i, h)),
        out_shape=jax.ShapeDtypeStruct((T, W), F32),
        scratch_shapes=scratch,
        compiler_params=_params("parallel", "arbitrary"),
        name="hgrn_fwd",
    )(proj, proj, proj, lb_f.reshape(1, W))

    last = n_tiles - 1
    rtile = lambda blk: pl.BlockSpec((HG_TILE, GW), lambda h, i: (last - i, blk + h))
    return pl.pallas_call(
        _hgrn_bwd_kernel,
        grid=(n_groups, n_tiles),
        in_specs=[rtile(cq), rtile(czb), rtile(ci), vec, rtile(0), rtile(cg), vec],
        out_specs=pl.BlockSpec((HG_TILE, GW), lambda h, i: (last - i, h)),
        out_shape=jax.ShapeDtypeStruct((T, W), BF16),
        scratch_shapes=scratch,
        compiler_params=_params("parallel", "arbitrary"),
        name="hgrn_bwd",
    )(proj, proj, proj, lb_b.reshape(1, W), o_fwd, proj, norm_w.reshape(1, W))


def _rope_tables(T):
    rows = T // GRID_W
    row = jnp.repeat(jnp.arange(rows, dtype=F32), GRID_W)
    col = jnp.tile(jnp.arange(GRID_W, dtype=F32), rows)
    axis_dim = HEAD // 2
    inv = ROPE_THETA ** (-jnp.arange(0, axis_dim, 2, dtype=F32) / axis_dim)
    ang = jnp.concatenate([row[:, None] * inv, col[:, None] * inv], axis=-1)
    cos = jnp.repeat(jnp.cos(ang), 2, axis=-1)
    sin = jnp.repeat(jnp.sin(ang), 2, axis=-1)
    sign = jnp.tile(jnp.array([-1.0, 1.0], F32), HEAD // 2)
    return cos, sin * sign


def _norm_rope_head(x, w, cos, sin_signed, scale, axis):
    x = x * lax.rsqrt(jnp.mean(x * x, axis=axis, keepdims=True) + EPS) * w
    pos = lax.broadcasted_iota(jnp.int32, x.shape, axis)
    partner = jnp.where(pos % 2 == 0,
                        pltpu.roll(x, HEAD - 1, axis=axis),
                        pltpu.roll(x, 1, axis=axis))
    out = x * cos + partner * sin_signed
    return out * scale if scale is not None else out


def _qk_rope_kernel(q_ref, k_ref, v_ref, qw_ref, kw_ref, cos_ref, sin_ref, cos_t_ref, sin_t_ref,
                    qt_ref, ko_ref, vt_ref):
    cos_t, sin_t = cos_t_ref[...], sin_t_ref[...]
    for h in range(ATT_Q_HEADS):
        sl = slice(h * HEAD, (h + 1) * HEAD)
        q = _norm_rope_head(q_ref[:, sl].T, qw_ref[...], cos_t, sin_t, ATT_Q_SCALE, axis=0)
        qt_ref[sl, :] = q.astype(qt_ref.dtype)
    cos, sin = cos_ref[...], sin_ref[...]
    for h in range(ATT_KV_HEADS):
        sl = slice(h * HEAD, (h + 1) * HEAD)
        ko_ref[:, sl] = _norm_rope_head(k_ref[:, sl], kw_ref[...], cos, sin, None,
                                        axis=1).astype(ko_ref.dtype)
        vt_ref[0, sl, :] = v_ref[:, sl].T.astype(vt_ref.dtype)


def qk_norm_rope(proj, q_w, k_w, cols):
    T = proj.shape[0]
    tm = ATT_SUB_K
    assert T % tm == 0 and T % GRID_W == 0
    QW, KW = ATT_Q_HEADS * HEAD, ATT_KV_HEADS * HEAD
    cq, ck, cv = cols
    cos, sin = _rope_tables(T)
    q_w_col = jnp.broadcast_to(q_w.reshape(HEAD, 1), (HEAD, tm))
    return pl.pallas_call(
        _qk_rope_kernel,
        grid=(T // tm,),
        in_specs=[pl.BlockSpec((tm, QW), lambda i: (i, cq // QW)),
                  pl.BlockSpec((tm, KW), lambda i: (i, ck // KW)),
                  pl.BlockSpec((tm, KW), lambda i: (i, cv // KW)),
                  pl.BlockSpec((HEAD, tm), lambda i: (0, 0)),
                  pl.BlockSpec((1, HEAD), lambda i: (0, 0)),
                  pl.BlockSpec((tm, HEAD), lambda i: (i, 0)),
                  pl.BlockSpec((tm, HEAD), lambda i: (i, 0)),
                  pl.BlockSpec((HEAD, tm), lambda i: (0, i)),
                  pl.BlockSpec((HEAD, tm), lambda i: (0, i))],
        out_specs=[pl.BlockSpec((QW, tm), lambda i: (0, i)),
                   pl.BlockSpec((tm, KW), lambda i: (i, 0)),
                   pl.BlockSpec((1, KW, tm), lambda i: (i, 0, 0))],
        out_shape=[jax.ShapeDtypeStruct((QW, T), BF16),
                   jax.ShapeDtypeStruct((T, KW), BF16),
                   jax.ShapeDtypeStruct((T // tm, KW, tm), BF16)],
        compiler_params=_params("parallel"),
        name="qk_norm_rope",
    )(proj, proj, proj, q_w_col, k_w.reshape(1, HEAD), cos, sin, cos.T, sin.T)


def _row_norm_bound(norm_w, scale):
    margin = 1.0 + 2.0 ** -7
    return jnp.sqrt(jnp.float32(HEAD)) * jnp.max(jnp.abs(norm_w)) * (scale * margin)


def _flash_kernel(bounded_ref, qt_ref, k_ref, vt_ref, kmax_ref, o_ref, m_scr, l_scr, acc_scr):
    j = pl.program_id(2)
    group = ATT_Q_HEADS // ATT_KV_HEADS
    n_sub = k_ref.shape[0] // ATT_SUB_K
    bounded = bounded_ref[0] != 0

    def key_block(jj):
        k = k_ref[pl.ds(pl.multiple_of(jj * ATT_SUB_K, ATT_SUB_K), ATT_SUB_K), :]
        return k, vt_ref[jj]

    def logits(k, r):
        return jnp.dot(k, qt_ref[r * HEAD:(r + 1) * HEAD, :], preferred_element_type=F32)

    @pl.when(j == 0)
    def _():
        l_scr[...] = jnp.zeros_like(l_scr)
        acc_scr[...] = jnp.zeros_like(acc_scr)

        @pl.when(bounded)
        def _():
            for r in range(group):
                q = qt_ref[r * HEAD:(r + 1) * HEAD, :].astype(F32)
                m_scr[r] = jnp.sqrt(jnp.sum(q * q, axis=0, keepdims=True)) * kmax_ref[...]

        @pl.when(jnp.logical_not(bounded))
        def _():
            m_scr[...] = jnp.full_like(m_scr, -jnp.inf)

    @pl.when(bounded)
    def _():
        stages = [(jj, r) for jj in range(n_sub) for r in range(group)]

        def stage_logits(stage):
            jj, r = stage
            return logits(k_ref[jj * ATT_SUB_K:(jj + 1) * ATT_SUB_K, :], r)

        st = stage_logits(stages[0])
        for idx, (jj, r) in enumerate(stages):
            st_next = stage_logits(stages[idx + 1]) if idx + 1 < len(stages) else None
            pt = jnp.exp2(st - m_scr[r])
            l_scr[r] += jnp.sum(pt, axis=0, keepdims=True)
            acc_scr[r] += jnp.dot(vt_ref[jj], pt.astype(BF16), preferred_element_type=F32)
            st = st_next

    @pl.when(jnp.logical_not(bounded))
    def _():
        def sub_block(jj, carry):
            k, vt = key_block(jj)
            for r in range(group):
                st = logits(k, r)
                m_prev = m_scr[r]
                m_new = jnp.maximum(m_prev, jnp.max(st, axis=0, keepdims=True))
                alpha = jnp.exp2(m_prev - m_new)
                pt = jnp.exp2(st - m_new)
                l_scr[r] = alpha * l_scr[r] + jnp.sum(pt, axis=0, keepdims=True)
                acc_scr[r] = alpha * acc_scr[r] + jnp.dot(vt, pt.astype(BF16),
                                                          preferred_element_type=F32)
                m_scr[r] = m_new
            return carry

        lax.fori_loop(0, n_sub, sub_block, 0)

    @pl.when(j == pl.num_programs(2) - 1)
    def _():
        for r in range(group):
            o = acc_scr[r] / l_scr[r]
            o_ref[:, r * HEAD:(r + 1) * HEAD] = o.T.astype(o_ref.dtype)


def flash_attention(qt, k, vt, q_max, k_max, tq=ATT_BLOCK_Q, tk=ATT_BLOCK_K):
    T = k.shape[0]
    tk = min(tk, T)
    assert T % tq == 0 and T % tk == 0 and tk % ATT_SUB_K == 0
    group = ATT_Q_HEADS // ATT_KV_HEADS
    GW = group * HEAD
    bounded = (q_max * k_max <= ATT_MAX_BOUNDED_LOGIT).astype(jnp.int32).reshape(1)
    kmax_row = jnp.full((1, tq), k_max, F32)
    grid_spec = pltpu.PrefetchScalarGridSpec(
        num_scalar_prefetch=1,
        grid=(ATT_KV_HEADS, T // tq, T // tk),
        in_specs=[pl.BlockSpec((GW, tq), lambda g, i, j, b: (g, i)),
                  pl.BlockSpec((tk, HEAD), lambda g, i, j, b: (j, g)),
                  pl.BlockSpec((tk // ATT_SUB_K, HEAD, ATT_SUB_K), lambda g, i, j, b: (j, g, 0)),
                  pl.BlockSpec((1, tq), lambda g, i, j, b: (0, 0))],
        out_specs=pl.BlockSpec((tq, GW), lambda g, i, j, b: (i, g)),
        scratch_shapes=[pltpu.VMEM((group, 1, tq), F32),
                        pltpu.VMEM((group, 1, tq), F32),
                        pltpu.VMEM((group, HEAD, tq), F32)])
    return pl.pallas_call(
        _flash_kernel,
        grid_spec=grid_spec,
        out_shape=jax.ShapeDtypeStruct((T, ATT_Q_HEADS * HEAD), BF16),
        compiler_params=_params("parallel", "parallel", "arbitrary"),
        name="flash_attention",
    )(bounded, qt, k, vt, kmax_row)


def _encoder_layer(x, norm_mix_w, w_in, lb_f, lb_b, hg_norm_w, q_norm_w, k_norm_w,
                   w_branch_a, w_branch_b, w_out, norm_ffn_w, w_gate_up, w_down):
    hg_w = HG_HEADS * HEAD
    q_w, kv_w = ATT_Q_HEADS * HEAD, ATT_KV_HEADS * HEAD
    D = x.shape[1]
    sizes = (hg_w, hg_w, hg_w, hg_w, hg_w, q_w, kv_w, kv_w, D, D)
    offs = np.concatenate([[0], np.cumsum(sizes)]).tolist()
    assert offs[-1] == w_in.shape[1]

    h = rmsnorm(x, norm_mix_w, BF16)
    proj = in_projection(h, w_in)
    o_hg = hgrn2_bidirectional(proj, lb_f, lb_b, hg_norm_w, offs[0:5])
    qt, k, vt = qk_norm_rope(proj, q_norm_w, k_norm_w, offs[5:8])
    o_att = flash_attention(qt, k, vt, _row_norm_bound(q_norm_w, ATT_Q_SCALE),
                            _row_norm_bound(k_norm_w, 1.0))
    merged = gated_merge(o_hg, o_att, w_branch_a, w_branch_b, proj, offs[8], offs[9])
    x = residual_matmul(merged, w_out, x, OUT_PROJ_TILE, "out_projection")
    h = rmsnorm(x, norm_ffn_w, BF16)
    act = swiglu_up(h, w_gate_up)
    return residual_matmul(act, w_down, x, SWIGLU_DOWN_TILE, "swiglu_down")


def kernel(x_prompt, x_sample, norm_mix_w, w_in, lb_fwd, lb_bwd, hg_norm_w, q_norm_w, k_norm_w,
           w_branch_a, w_branch_b, w_out, norm_ffn_w, w_gate_up, w_down, norm_final_w):
    depth = w_in.shape[0]
    lbs_f = jnp.cumsum(jax.nn.softmax(lb_fwd.astype(F32), axis=0), axis=0)
    lbs_b = jnp.cumsum(jax.nn.softmax(lb_bwd.astype(F32), axis=0), axis=0)
    bf = lambda w: w.astype(BF16)
    w_in, w_branch_a, w_branch_b, w_out, w_gate_up, w_down = map(
        bf, (w_in, w_branch_a, w_branch_b, w_out, w_gate_up, w_down))

    def trunk(x):
        x = x.reshape(x.shape[-2], x.shape[-1])
        for l in range(depth):
            x = _encoder_layer(x, norm_mix_w[l], w_in[l], lbs_f[l], lbs_b[l], hg_norm_w[l],
                               q_norm_w[l], k_norm_w[l], w_branch_a[l], w_branch_b[l], w_out[l],
                               norm_ffn_w[l], w_gate_up[l], w_down[l])
        return rmsnorm(x, norm_final_w, F32)

    assert x_prompt.shape[0] == 1 and x_sample.shape[0] == 1
    return trunk(x_prompt)[None], trunk(x_sample)[None]
```

```python
import functools

import jax
import jax.numpy as jnp
import numpy as np
from jax import lax
from jax.experimental import pallas as pl
from jax.experimental.pallas import tpu as pltpu

F32 = jnp.float32
BF16 = jnp.bfloat16

EPS = 1e-6
LANES = 128
BF16_SUBLANES = 16
HEAD = 128
HG_HEADS = 16
ATT_Q_HEADS = 16
ATT_KV_HEADS = 4
GRID_W = 64
ROPE_THETA = 10000.0

V7X_VMEM_LIMIT_BYTES = 56 * 1024 * 1024

IN_PROJ_TILE = (1024, 1024)
MERGE_TILE = (1024, 512)
OUT_PROJ_TILE = (1024, 1024)
SWIGLU_UP_TILE = (2048, 256)
SWIGLU_DOWN_TILE = (512, 512)
NORM_ROWS = 512

HG_CHUNK = 64
HG_TILE = 256
HG_GROUP = 16
HG_MAX_EXPONENT = 115.0

ATT_SUB_K = 512
ATT_BLOCK_Q = 512
ATT_BLOCK_K = 8192
ATT_Q_SCALE = float(np.log2(np.e) / np.sqrt(np.float32(HEAD)))
ATT_MAX_BOUNDED_LOGIT = 60.0


def _params(*semantics):
    return pltpu.CompilerParams(dimension_semantics=semantics,
                                vmem_limit_bytes=V7X_VMEM_LIMIT_BYTES)


def _sigmoid(x):
    return 0.5 + 0.5 * jnp.tanh(0.5 * x)


def _silu(x):
    return x * _sigmoid(x)


def _rmsnorm_kernel(x_ref, w_ref, o_ref):
    x = x_ref[...]
    inv = lax.rsqrt(jnp.mean(x * x, axis=-1, keepdims=True) + EPS)
    o_ref[...] = (x * inv * w_ref[...]).astype(o_ref.dtype)


def rmsnorm(x, w, out_dtype):
    T, D = x.shape
    tm = NORM_ROWS
    assert T % tm == 0
    return pl.pallas_call(
        _rmsnorm_kernel,
        grid=(T // tm,),
        in_specs=[pl.BlockSpec((tm, D), lambda i: (i, 0)),
                  pl.BlockSpec((1, D), lambda i: (0, 0))],
        out_specs=pl.BlockSpec((tm, D), lambda i: (i, 0)),
        out_shape=jax.ShapeDtypeStruct((T, D), out_dtype),
        compiler_params=_params("parallel"),
        name="rmsnorm",
    )(x, w.reshape(1, D))


def _proj_kernel(a_ref, w_ref, o_ref):
    o_ref[...] = jnp.dot(a_ref[...], w_ref[...], preferred_element_type=F32)


def in_projection(h, w):
    tm, tn = IN_PROJ_TILE
    T, K = h.shape
    N = w.shape[1]
    assert T % tm == 0 and N % tn == 0
    return pl.pallas_call(
        _proj_kernel,
        grid=(T // tm, N // tn),
        in_specs=[pl.BlockSpec((tm, K), lambda i, j: (i, 0)),
                  pl.BlockSpec((K, tn), lambda i, j: (0, j))],
        out_specs=pl.BlockSpec((tm, tn), lambda i, j: (i, j)),
        out_shape=jax.ShapeDtypeStruct((T, N), F32),
        compiler_params=_params("parallel", "arbitrary"),
        name="in_projection",
    )(h, w)


def _merge_kernel(oa_ref, ob_ref, wa_ref, wb_ref, ga_ref, gb_ref, o_ref):
    ya = jnp.dot(oa_ref[...], wa_ref[...], preferred_element_type=F32)
    yb = jnp.dot(ob_ref[...], wb_ref[...], preferred_element_type=F32)
    o_ref[...] = (_sigmoid(ga_ref[...]) * ya + _sigmoid(gb_ref[...]) * yb).astype(o_ref.dtype)


def gated_merge(o_hg, o_att, w_a, w_b, proj, ga_col, gb_col):
    tm, tn = MERGE_TILE
    T, K = o_hg.shape
    D = w_a.shape[1]
    assert T % tm == 0 and D % tn == 0 and ga_col % tn == 0 and gb_col % tn == 0
    ga_blk, gb_blk = ga_col // tn, gb_col // tn
    return pl.pallas_call(
        _merge_kernel,
        grid=(T // tm, D // tn),
        in_specs=[pl.BlockSpec((tm, K), lambda i, j: (i, 0)),
                  pl.BlockSpec((tm, K), lambda i, j: (i, 0)),
                  pl.BlockSpec((K, tn), lambda i, j: (0, j)),
                  pl.BlockSpec((K, tn), lambda i, j: (0, j)),
                  pl.BlockSpec((tm, tn), lambda i, j: (i, ga_blk + j)),
                  pl.BlockSpec((tm, tn), lambda i, j: (i, gb_blk + j))],
        out_specs=pl.BlockSpec((tm, tn), lambda i, j: (i, j)),
        out_shape=jax.ShapeDtypeStruct((T, D), BF16),
        compiler_params=_params("parallel", "arbitrary"),
        name="gated_merge",
    )(o_hg, o_att, w_a, w_b, proj, proj)


def _residual_mm_kernel(a_ref, w_ref, r_ref, o_ref):
    o_ref[...] = r_ref[...] + jnp.dot(a_ref[...], w_ref[...], preferred_element_type=F32)


def residual_matmul(a, w, res, tile, name):
    tm, tn = tile
    T, K = a.shape
    N = w.shape[1]
    assert T % tm == 0 and N % tn == 0
    return pl.pallas_call(
        _residual_mm_kernel,
        grid=(T // tm, N // tn),
        in_specs=[pl.BlockSpec((tm, K), lambda i, j: (i, 0)),
                  pl.BlockSpec((K, tn), lambda i, j: (0, j)),
                  pl.BlockSpec((tm, tn), lambda i, j: (i, j))],
        out_specs=pl.BlockSpec((tm, tn), lambda i, j: (i, j)),
        out_shape=jax.ShapeDtypeStruct((T, N), F32),
        compiler_params=_params("parallel", "arbitrary"),
        name=name,
    )(a, w, res)


def _swiglu_up_kernel(a_ref, wg_ref, wu_ref, o_ref):
    a = a_ref[...]
    g = jnp.dot(a, wg_ref[...], preferred_element_type=F32)
    u = jnp.dot(a, wu_ref[...], preferred_element_type=F32)
    o_ref[...] = (_silu(g) * u).astype(o_ref.dtype)


def swiglu_up(h, w_gate_up):
    tm, tn = SWIGLU_UP_TILE
    T, K = h.shape
    F = w_gate_up.shape[1] // 2
    assert T % tm == 0 and F % tn == 0
    u_blk = F // tn
    return pl.pallas_call(
        _swiglu_up_kernel,
        grid=(T // tm, F // tn),
        in_specs=[pl.BlockSpec((tm, K), lambda i, j: (i, 0)),
                  pl.BlockSpec((K, tn), lambda i, j: (0, j)),
                  pl.BlockSpec((K, tn), lambda i, j: (0, u_blk + j))],
        out_specs=pl.BlockSpec((tm, tn), lambda i, j: (i, j)),
        out_shape=jax.ShapeDtypeStruct((T, F), BF16),
        compiler_params=_params("parallel", "arbitrary"),
        name="swiglu_up",
    )(h, w_gate_up, w_gate_up)


def _split2_bf16(x):
    hi = x.astype(BF16)
    return hi, (x - hi.astype(F32)).astype(BF16)


def _forget_gate(z, lb):
    half = 0.5 * (1.0 - lb)
    kk = half - half * jnp.tanh(0.5 * z)
    f = 1.0 - kk
    return f, jnp.log2(f), kk


def _hgrn_scan_tile(q_ref, z_ref, v_ref, lb_ref, st_ref, st_save, o_scr, vt_scr, emit, reverse):
    C = HG_CHUNK
    n_chunks = HG_TILE // C
    chunks = [slice(j * C, (j + 1) * C) for j in range(n_chunks)]
    row = lax.broadcasted_iota(jnp.int32, (HG_TILE, HG_TILE), 0)
    col = lax.broadcasted_iota(jnp.int32, (HG_TILE, HG_TILE), 1)
    visible = (col >= row) if reverse else (col <= row)
    chunk_bits = C.bit_length() - 1
    assert C == 1 << chunk_bits
    same_chunk = lax.shift_right_logical(row, chunk_bits) == lax.shift_right_logical(col, chunk_bits)
    keep = visible & same_chunk
    tri = jnp.where(keep, 1.0, 0.0).astype(BF16)
    end = 0 if reverse else C - 1
    mid = C // 2 if reverse else C // 2 - 1
    heads = [slice(g * HEAD, (g + 1) * HEAD) for g in range(HG_GROUP)]
    nt_dims = (((1,), (1,)), ((), ()))

    def rel(c, off):
        return jnp.concatenate([c[sl] - c[sl.start + off:sl.start + off + 1] for sl in chunks],
                               axis=0)

    st_save[...] = st_ref[...]

    keys, cums = [], []
    for hs in heads:
        _, logf, kk = _forget_gate(z_ref[:, hs], lb_ref[:, hs])
        keys.append(kk)
        cs = jnp.dot(tri, jnp.concatenate(_split2_bf16(logf), axis=1),
                     preferred_element_type=F32)
        cums.append(cs[:, :HEAD] + cs[:, HEAD:])
    mids = [rel(c, mid) for c in cums]
    worst = functools.reduce(jnp.maximum, [jnp.max(jnp.abs(a)) for a in mids])
    scores = []
    for g, hs in enumerate(heads):
        q_in = (q_ref[:, hs] * jnp.exp2(mids[g])).astype(BF16)
        k_in = (keys[g] * jnp.exp2(-mids[g])).astype(BF16)
        scores.append(lax.dot_general(q_in, k_in, nt_dims, preferred_element_type=F32))
    q_state, decay, update = [], {}, {}
    for g, hs in enumerate(heads):
        c, v = cums[g], v_ref[:, hs]
        k_st = (keys[g] * jnp.exp2(-rel(c, end))).astype(BF16)
        q_state.append((q_ref[:, hs] * jnp.exp2(c)).astype(BF16))
        for j, sl in enumerate(chunks):
            update[g, j] = jnp.dot(v[sl].T.astype(BF16), k_st[sl], preferred_element_type=F32)
            decay[g, j] = jnp.exp2(c[sl.start + end:sl.start + end + 1])
    for g, hs in enumerate(heads):
        s = jnp.where(keep, scores[g], 0.0).astype(BF16)
        o_scr[:, hs] = jnp.dot(s, v_ref[:, hs].astype(BF16), preferred_element_type=F32)
    order = range(n_chunks - 1, -1, -1) if reverse else range(n_chunks)
    for j in order:
        sl = chunks[j]
        for g, hs in enumerate(heads):
            st = st_ref[g]
            o_scr[sl, hs] += lax.dot_general(q_state[g][sl], st.astype(BF16), nt_dims,
                                             preferred_element_type=F32)
            st_ref[g] = st * decay[g, j] + update[g, j]
    emit()

    @pl.when(jnp.logical_not(worst < HG_MAX_EXPONENT))
    def _():
        st_ref[...] = st_save[...]
        lane = lax.broadcasted_iota(jnp.int32, (1, HG_TILE), 1)
        rows = lax.broadcasted_iota(jnp.int32, (HG_TILE, 1), 0)
        for g, hs in enumerate(heads):
            z, lb = z_ref[:, hs], lb_ref[:, hs]
            q = q_ref[:, hs]
            vt_scr[...] = v_ref[:, hs].T

            def body(i, carry):
                t = (HG_TILE - 1 - i) if reverse else i
                onehot = (lane == t).astype(F32)
                pick = lambda x: jnp.sum(jnp.where(rows == t, x, 0.0), axis=0, keepdims=True)
                f_t, _, k_t = _forget_gate(pick(z), lb)
                q_t = pick(q)
                v_col = jnp.sum(vt_scr[...] * onehot, axis=1, keepdims=True)
                st = st_ref[g] * f_t + v_col * k_t
                st_ref[g] = st
                o_col = jnp.sum(st * q_t, axis=1, keepdims=True)
                return carry + o_col * onehot

            ot = lax.fori_loop(0, HG_TILE, body, jnp.zeros((HEAD, HG_TILE), F32))
            o_scr[:, hs] = ot.T
        emit()


def _hgrn_fwd_kernel(q_ref, z_ref, v_ref, lb_ref, o_ref, st_ref, st_save, o_scr, vt_scr):
    @pl.when(pl.program_id(1) == 0)
    def _():
        st_ref[...] = jnp.zeros_like(st_ref)

    def emit():
        o_ref[...] = o_scr[...]

    _hgrn_scan_tile(q_ref, z_ref, v_ref, lb_ref, st_ref, st_save, o_scr, vt_scr, emit,
                    reverse=False)


def _hgrn_bwd_kernel(q_ref, z_ref, v_ref, lb_ref, of_ref, gate_ref, nw_ref, o_ref,
                     st_ref, st_save, o_scr, vt_scr):
    @pl.when(pl.program_id(1) == 0)
    def _():
        st_ref[...] = jnp.zeros_like(st_ref)

    def emit():
        for g in range(HG_GROUP):
            hs = slice(g * HEAD, (g + 1) * HEAD)
            o = of_ref[:, hs] + o_scr[:, hs]
            o = o * lax.rsqrt(jnp.mean(o * o, axis=-1, keepdims=True) + EPS) * nw_ref[:, hs]
            o_ref[:, hs] = (o * _silu(gate_ref[:, hs])).astype(o_ref.dtype)

    _hgrn_scan_tile(q_ref, z_ref, v_ref, lb_ref, st_ref, st_save, o_scr, vt_scr, emit,
                    reverse=True)


def hgrn2_bidirectional(proj, lb_f, lb_b, norm_w, cols):
    T = proj.shape[0]
    assert T % HG_TILE == 0 and HG_HEADS % HG_GROUP == 0
    n_tiles = T // HG_TILE
    W = HG_HEADS * HEAD
    GW = HG_GROUP * HEAD
    n_groups = HG_HEADS // HG_GROUP
    cq, czf, czb, ci, cg = (c // GW for c in cols)
    scratch = [pltpu.VMEM((HG_GROUP, HEAD, HEAD), F32),
               pltpu.VMEM((HG_GROUP, HEAD, HEAD), F32),
               pltpu.VMEM((HG_TILE, GW), F32),
               pltpu.VMEM((HEAD, HG_TILE), F32)]
    tile = lambda blk: pl.BlockSpec((HG_TILE, GW), lambda h, i: (i, blk + h))
    vec = pl.BlockSpec((1, GW), lambda h, i: (0, h))
    o_fwd = pl.pallas_call(
        _hgrn_fwd_kernel,
        grid=(n_groups, n_tiles),
        in_specs=[tile(cq), tile(czf), tile(ci), vec],
        out_specs=pl.BlockSpec((HG_TILE, GW), lambda h, i: (i, h)),
        out_shape=jax.ShapeDtypeStruct((T, W), F32),
        scratch_shapes=scratch,
        compiler_params=_params("parallel", "arbitrary"),
        name="hgrn_fwd",
    )(proj, proj, proj, lb_f.reshape(1, W))

    last = n_tiles - 1
    rtile = lambda blk: pl.BlockSpec((HG_TILE, GW), lambda h, i: (last - i, blk + h))
    return pl.pallas_call(
        _hgrn_bwd_kernel,
        grid=(n_groups, n_tiles),
        in_specs=[rtile(cq), rtile(czb), rtile(ci), vec, rtile(0), rtile(cg), vec],
        out_specs=pl.BlockSpec((HG_TILE, GW), lambda h, i: (last - i, h)),
        out_shape=jax.ShapeDtypeStruct((T, W), BF16),
        scratch_shapes=scratch,
        compiler_params=_params("parallel", "arbitrary"),
        name="hgrn_bwd",
    )(proj, proj, proj, lb_b.reshape(1, W), o_fwd, proj, norm_w.reshape(1, W))


def _rope_tables(T):
    rows = T // GRID_W
    row = jnp.repeat(jnp.arange(rows, dtype=F32), GRID_W)
    col = jnp.tile(jnp.arange(GRID_W, dtype=F32), rows)
    axis_dim = HEAD // 2
    inv = ROPE_THETA ** (-jnp.arange(0, axis_dim, 2, dtype=F32) / axis_dim)
    ang = jnp.concatenate([row[:, None] * inv, col[:, None] * inv], axis=-1)
    cos = jnp.repeat(jnp.cos(ang), 2, axis=-1)
    sin = jnp.repeat(jnp.sin(ang), 2, axis=-1)
    sign = jnp.tile(jnp.array([-1.0, 1.0], F32), HEAD // 2)
    return cos, sin * sign


def _norm_rope_head(x, w, cos, sin_signed, scale, axis):
    x = x * lax.rsqrt(jnp.mean(x * x, axis=axis, keepdims=True) + EPS) * w
    pos = lax.broadcasted_iota(jnp.int32, x.shape, axis)
    partner = jnp.where(pos % 2 == 0,
                        pltpu.roll(x, HEAD - 1, axis=axis),
                        pltpu.roll(x, 1, axis=axis))
    out = x * cos + partner * sin_signed
    return out * scale if scale is not None else out


def _qk_rope_kernel(q_ref, k_ref, v_ref, qw_ref, kw_ref, cos_ref, sin_ref, cos_t_ref, sin_t_ref,
                    qt_ref, ko_ref, vt_ref):
    cos_t, sin_t = cos_t_ref[...], sin_t_ref[...]
    for h in range(ATT_Q_HEADS):
        sl = slice(h * HEAD, (h + 1) * HEAD)
        q = _norm_rope_head(q_ref[:, sl].T, qw_ref[...], cos_t, sin_t, ATT_Q_SCALE, axis=0)
        qt_ref[sl, :] = q.astype(qt_ref.dtype)
    cos, sin = cos_ref[...], sin_ref[...]
    for h in range(ATT_KV_HEADS):
        sl = slice(h * HEAD, (h + 1) * HEAD)
        ko_ref[:, sl] = _norm_rope_head(k_ref[:, sl], kw_ref[...], cos, sin, None,
                                        axis=1).astype(ko_ref.dtype)
        vt_ref[0, sl, :] = v_ref[:, sl].T.astype(vt_ref.dtype)


def qk_norm_rope(proj, q_w, k_w, cols):
    T = proj.shape[0]
    tm = ATT_SUB_K
    assert T % tm == 0 and T % GRID_W == 0
    QW, KW = ATT_Q_HEADS * HEAD, ATT_KV_HEADS * HEAD
    cq, ck, cv = cols
    cos, sin = _rope_tables(T)
    q_w_col = jnp.broadcast_to(q_w.reshape(HEAD, 1), (HEAD, tm))
    return pl.pallas_call(
        _qk_rope_kernel,
        grid=(T // tm,),
        in_specs=[pl.BlockSpec((tm, QW), lambda i: (i, cq // QW)),
                  pl.BlockSpec((tm, KW), lambda i: (i, ck // KW)),
                  pl.BlockSpec((tm, KW), lambda i: (i, cv // KW)),
                  pl.BlockSpec((HEAD, tm), lambda i: (0, 0)),
                  pl.BlockSpec((1, HEAD), lambda i: (0, 0)),
                  pl.BlockSpec((tm, HEAD), lambda i: (i, 0)),
                  pl.BlockSpec((tm, HEAD), lambda i: (i, 0)),
                  pl.BlockSpec((HEAD, tm), lambda i: (0, i)),
                  pl.BlockSpec((HEAD, tm), lambda i: (0, i))],
        out_specs=[pl.BlockSpec((QW, tm), lambda i: (0, i)),
                   pl.BlockSpec((tm, KW), lambda i: (i, 0)),
                   pl.BlockSpec((1, KW, tm), lambda i: (i, 0, 0))],
        out_shape=[jax.ShapeDtypeStruct((QW, T), BF16),
                   jax.ShapeDtypeStruct((T, KW), BF16),
                   jax.ShapeDtypeStruct((T // tm, KW, tm), BF16)],
        compiler_params=_params("parallel"),
        name="qk_norm_rope",
    )(proj, proj, proj, q_w_col, k_w.reshape(1, HEAD), cos, sin, cos.T, sin.T)


def _row_norm_bound(norm_w, scale):
    margin = 1.0 + 2.0 ** -7
    return jnp.sqrt(jnp.float32(HEAD)) * jnp.max(jnp.abs(norm_w)) * (scale * margin)


def _flash_kernel(bounded_ref, qt_ref, k_ref, vt_ref, kmax_ref, *refs, n_side):
    side_in, o_ref, side_out = refs[:n_side], refs[n_side], refs[n_side + 1:2 * n_side + 1]
    m_scr, l_scr, acc_scr = refs[2 * n_side + 1:]
    j = pl.program_id(2)
    group = ATT_Q_HEADS // ATT_KV_HEADS

    def cast_side():
        for src, dst in zip(side_in, side_out):
            dst[...] = src[...].astype(dst.dtype)

    n_sub = k_ref.shape[0] // ATT_SUB_K
    bounded = bounded_ref[0] != 0

    def key_block(jj):
        k = k_ref[pl.ds(pl.multiple_of(jj * ATT_SUB_K, ATT_SUB_K), ATT_SUB_K), :]
        return k, vt_ref[jj]

    def logits(k, r):
        return jnp.dot(k, qt_ref[r * HEAD:(r + 1) * HEAD, :], preferred_element_type=F32)

    @pl.when(j == 0)
    def _():
        l_scr[...] = jnp.zeros_like(l_scr)
        acc_scr[...] = jnp.zeros_like(acc_scr)

        @pl.when(bounded)
        def _():
            for r in range(group):
                q = qt_ref[r * HEAD:(r + 1) * HEAD, :].astype(F32)
                m_scr[r] = jnp.sqrt(jnp.sum(q * q, axis=0, keepdims=True)) * kmax_ref[...]

        @pl.when(jnp.logical_not(bounded))
        def _():
            m_scr[...] = jnp.full_like(m_scr, -jnp.inf)

    @pl.when(bounded)
    def _():
        stages = [(jj, r) for jj in range(n_sub) for r in range(group)]
        cast_side()

        def stage_logits(stage):
            jj, r = stage
            return logits(k_ref[jj * ATT_SUB_K:(jj + 1) * ATT_SUB_K, :], r)

        st = stage_logits(stages[0])
        for idx, (jj, r) in enumerate(stages):
            st_next = stage_logits(stages[idx + 1]) if idx + 1 < len(stages) else None
            pt = jnp.exp2(st - m_scr[r])
            l_scr[r] += jnp.sum(pt, axis=0, keepdims=True)
            acc_scr[r] += jnp.dot(vt_ref[jj], pt.astype(BF16), preferred_element_type=F32)
            st = st_next

    @pl.when(jnp.logical_not(bounded))
    def _():
        cast_side()

        def sub_block(jj, carry):
            k, vt = key_block(jj)
            for r in range(group):
                st = logits(k, r)
                m_prev = m_scr[r]
                m_new = jnp.maximum(m_prev, jnp.max(st, axis=0, keepdims=True))
                alpha = jnp.exp2(m_prev - m_new)
                pt = jnp.exp2(st - m_new)
                l_scr[r] = alpha * l_scr[r] + jnp.sum(pt, axis=0, keepdims=True)
                acc_scr[r] = alpha * acc_scr[r] + jnp.dot(vt, pt.astype(BF16),
                                                          preferred_element_type=F32)
                m_scr[r] = m_new
            return carry

        lax.fori_loop(0, n_sub, sub_block, 0)

    @pl.when(j == pl.num_programs(2) - 1)
    def _():
        for r in range(group):
            o = acc_scr[r] / l_scr[r]
            o_ref[:, r * HEAD:(r + 1) * HEAD] = o.T.astype(o_ref.dtype)


def _slab_view(w, n_slabs):
    rows, cols = w.shape
    while rows % (n_slabs * BF16_SUBLANES) or cols % LANES:
        rows, cols = rows * 2, cols // 2
        assert cols >= LANES, w.shape
    return w.reshape(rows, cols)


def flash_attention(qt, k, vt, q_max, k_max, cast=(), tq=ATT_BLOCK_Q, tk=ATT_BLOCK_K):
    T = k.shape[0]
    tk = min(tk, T)
    assert T % tq == 0 and T % tk == 0 and tk % ATT_SUB_K == 0
    group = ATT_Q_HEADS // ATT_KV_HEADS
    GW = group * HEAD
    n_q = T // tq
    bounded = (q_max * k_max <= ATT_MAX_BOUNDED_LOGIT).astype(jnp.int32).reshape(1)
    kmax_row = jnp.full((1, tq), k_max, F32)
    views = [_slab_view(w, ATT_KV_HEADS * n_q) for w in cast]
    slab = lambda v: pl.BlockSpec((v.shape[0] // (ATT_KV_HEADS * n_q), v.shape[1]),
                                  lambda g, i, j, b: (g * n_q + i, 0))
    grid_spec = pltpu.PrefetchScalarGridSpec(
        num_scalar_prefetch=1,
        grid=(ATT_KV_HEADS, n_q, T // tk),
        in_specs=[pl.BlockSpec((GW, tq), lambda g, i, j, b: (g, i)),
                  pl.BlockSpec((tk, HEAD), lambda g, i, j, b: (j, g)),
                  pl.BlockSpec((tk // ATT_SUB_K, HEAD, ATT_SUB_K), lambda g, i, j, b: (j, g, 0)),
                  pl.BlockSpec((1, tq), lambda g, i, j, b: (0, 0))] + [slab(v) for v in views],
        out_specs=[pl.BlockSpec((tq, GW), lambda g, i, j, b: (i, g))] + [slab(v) for v in views],
        scratch_shapes=[pltpu.VMEM((group, 1, tq), F32),
                        pltpu.VMEM((group, 1, tq), F32),
                        pltpu.VMEM((group, HEAD, tq), F32)])
    out, *copies = pl.pallas_call(
        functools.partial(_flash_kernel, n_side=len(views)),
        grid_spec=grid_spec,
        out_shape=[jax.ShapeDtypeStruct((T, ATT_Q_HEADS * HEAD), BF16)]
                  + [jax.ShapeDtypeStruct(v.shape, BF16) for v in views],
        compiler_params=_params("parallel", "parallel", "arbitrary"),
        name="flash_attention",
    )(bounded, qt, k, vt, kmax_row, *views)
    return (out, *[c.reshape(w.shape) for c, w in zip(copies, cast)])


def _encoder_layer(x, norm_mix_w, w_in, lb_f, lb_b, hg_norm_w, q_norm_w, k_norm_w,
                   norm_ffn_w, late_weights):
    hg_w = HG_HEADS * HEAD
    q_w, kv_w = ATT_Q_HEADS * HEAD, ATT_KV_HEADS * HEAD
    D = x.shape[1]
    sizes = (hg_w, hg_w, hg_w, hg_w, hg_w, q_w, kv_w, kv_w, D, D)
    offs = np.concatenate([[0], np.cumsum(sizes)]).tolist()
    assert offs[-1] == w_in.shape[1]

    h = rmsnorm(x, norm_mix_w, BF16)
    proj = in_projection(h, w_in)
    o_hg = hgrn2_bidirectional(proj, lb_f, lb_b, hg_norm_w, offs[0:5])
    qt, k, vt = qk_norm_rope(proj, q_norm_w, k_norm_w, offs[5:8])
    to_cast = late_weights if late_weights[0].dtype == F32 else ()
    o_att, *converted = flash_attention(qt, k, vt, _row_norm_bound(q_norm_w, ATT_Q_SCALE),
                                        _row_norm_bound(k_norm_w, 1.0), cast=to_cast)
    late_weights = tuple(converted) if to_cast else late_weights
    w_branch_a, w_branch_b, w_out, w_gate_up, w_down = late_weights
    merged = gated_merge(o_hg, o_att, w_branch_a, w_branch_b, proj, offs[8], offs[9])
    x = residual_matmul(merged, w_out, x, OUT_PROJ_TILE, "out_projection")
    h = rmsnorm(x, norm_ffn_w, BF16)
    act = swiglu_up(h, w_gate_up)
    return residual_matmul(act, w_down, x, SWIGLU_DOWN_TILE, "swiglu_down"), late_weights


def kernel(x_prompt, x_sample, norm_mix_w, w_in, lb_fwd, lb_bwd, hg_norm_w, q_norm_w, k_norm_w,
           w_branch_a, w_branch_b, w_out, norm_ffn_w, w_gate_up, w_down, norm_final_w):
    depth = w_in.shape[0]
    lbs_f = jnp.cumsum(jax.nn.softmax(lb_fwd.astype(F32), axis=0), axis=0)
    lbs_b = jnp.cumsum(jax.nn.softmax(lb_bwd.astype(F32), axis=0), axis=0)
    w_in = w_in.astype(BF16)
    late = [(w_branch_a[l], w_branch_b[l], w_out[l], w_gate_up[l], w_down[l]) for l in range(depth)]

    def trunk(x):
        x = x.reshape(x.shape[-2], x.shape[-1])
        for l in range(depth):
            x, late[l] = _encoder_layer(x, norm_mix_w[l], w_in[l], lbs_f[l], lbs_b[l],
                                        hg_norm_w[l], q_norm_w[l], k_norm_w[l], norm_ffn_w[l],
                                        late[l])
        return rmsnorm(x, norm_final_w, F32)

    assert x_prompt.shape[0] == 1 and x_sample.shape[0] == 1
    return trunk(x_prompt)[None], trunk(x_sample)[None]
```

```python
import functools

import jax
import jax.numpy as jnp
import numpy as np
from jax import lax
from jax.experimental import pallas as pl
from jax.experimental.pallas import tpu as pltpu

F32 = jnp.float32
BF16 = jnp.bfloat16

EPS = 1e-6
LANES = 128
BF16_SUBLANES = 16
HEAD = 128
HG_HEADS = 16
ATT_Q_HEADS = 16
ATT_KV_HEADS = 4
GRID_W = 64
ROPE_THETA = 10000.0

V7X_VMEM_LIMIT_BYTES = 56 * 1024 * 1024

IN_PROJ_TILE = (1024, 1024)
MERGE_TILE = (1024, 512)
OUT_PROJ_TILE = (1024, 1024)
SWIGLU_UP_TILE = (2048, 256)
SWIGLU_DOWN_TILE = (512, 512)
NORM_ROWS = 512

HG_CHUNK = 64
HG_TILE = 256
HG_GROUP = 16
HG_MAX_EXPONENT = 115.0

ATT_SUB_K = 512
ATT_BLOCK_Q = 512
ATT_BLOCK_K = 8192
ATT_Q_SCALE = float(np.log2(np.e) / np.sqrt(np.float32(HEAD)))
ATT_MAX_BOUNDED_LOGIT = 60.0


def _params(*semantics):
    return pltpu.CompilerParams(dimension_semantics=semantics,
                                vmem_limit_bytes=V7X_VMEM_LIMIT_BYTES)


def _sigmoid(x):
    return 0.5 + 0.5 * jnp.tanh(0.5 * x)


def _silu(x):
    return x * _sigmoid(x)


def _rmsnorm_kernel(x_ref, w_ref, o_ref):
    x = x_ref[...]
    inv = lax.rsqrt(jnp.mean(x * x, axis=-1, keepdims=True) + EPS)
    o_ref[...] = (x * inv * w_ref[...]).astype(o_ref.dtype)


def rmsnorm(x, w, out_dtype):
    T, D = x.shape
    tm = NORM_ROWS
    assert T % tm == 0
    return pl.pallas_call(
        _rmsnorm_kernel,
        grid=(T // tm,),
        in_specs=[pl.BlockSpec((tm, D), lambda i: (i, 0)),
                  pl.BlockSpec((1, D), lambda i: (0, 0))],
        out_specs=pl.BlockSpec((tm, D), lambda i: (i, 0)),
        out_shape=jax.ShapeDtypeStruct((T, D), out_dtype),
        compiler_params=_params("parallel"),
        name="rmsnorm",
    )(x, w.reshape(1, D))


def _proj_kernel(a_ref, w_ref, o_ref):
    o_ref[...] = jnp.dot(a_ref[...], w_ref[...], preferred_element_type=F32)


def in_projection(h, w):
    tm, tn = IN_PROJ_TILE
    T, K = h.shape
    N = w.shape[1]
    assert T % tm == 0 and N % tn == 0
    return pl.pallas_call(
        _proj_kernel,
        grid=(T // tm, N // tn),
        in_specs=[pl.BlockSpec((tm, K), lambda i, j: (i, 0)),
                  pl.BlockSpec((K, tn), lambda i, j: (0, j))],
        out_specs=pl.BlockSpec((tm, tn), lambda i, j: (i, j)),
        out_shape=jax.ShapeDtypeStruct((T, N), F32),
        compiler_params=_params("parallel", "arbitrary"),
        name="in_projection",
    )(h, w)


def _merge_kernel(oa_ref, ob_ref, wa_ref, wb_ref, ga_ref, gb_ref, o_ref):
    ya = jnp.dot(oa_ref[...], wa_ref[...], preferred_element_type=F32)
    yb = jnp.dot(ob_ref[...], wb_ref[...], preferred_element_type=F32)
    o_ref[...] = (_sigmoid(ga_ref[...]) * ya + _sigmoid(gb_ref[...]) * yb).astype(o_ref.dtype)


def gated_merge(o_hg, o_att, w_a, w_b, proj, ga_col, gb_col):
    tm, tn = MERGE_TILE
    T, K = o_hg.shape
    D = w_a.shape[1]
    assert T % tm == 0 and D % tn == 0 and ga_col % tn == 0 and gb_col % tn == 0
    ga_blk, gb_blk = ga_col // tn, gb_col // tn
    return pl.pallas_call(
        _merge_kernel,
        grid=(T // tm, D // tn),
        in_specs=[pl.BlockSpec((tm, K), lambda i, j: (i, 0)),
                  pl.BlockSpec((tm, K), lambda i, j: (i, 0)),
                  pl.BlockSpec((K, tn), lambda i, j: (0, j)),
                  pl.BlockSpec((K, tn), lambda i, j: (0, j)),
                  pl.BlockSpec((tm, tn), lambda i, j: (i, ga_blk + j)),
                  pl.BlockSpec((tm, tn), lambda i, j: (i, gb_blk + j))],
        out_specs=pl.BlockSpec((tm, tn), lambda i, j: (i, j)),
        out_shape=jax.ShapeDtypeStruct((T, D), BF16),
        compiler_params=_params("parallel", "arbitrary"),
        name="gated_merge",
    )(o_hg, o_att, w_a, w_b, proj, proj)


def _residual_mm_kernel(a_ref, w_ref, r_ref, o_ref):
    o_ref[...] = r_ref[...] + jnp.dot(a_ref[...], w_ref[...], preferred_element_type=F32)


def residual_matmul(a, w, res, tile, name):
    tm, tn = tile
    T, K = a.shape
    N = w.shape[1]
    assert T % tm == 0 and N % tn == 0
    return pl.pallas_call(
        _residual_mm_kernel,
        grid=(T // tm, N // tn),
        in_specs=[pl.BlockSpec((tm, K), lambda i, j: (i, 0)),
                  pl.BlockSpec((K, tn), lambda i, j: (0, j)),
                  pl.BlockSpec((tm, tn), lambda i, j: (i, j))],
        out_specs=pl.BlockSpec((tm, tn), lambda i, j: (i, j)),
        out_shape=jax.ShapeDtypeStruct((T, N), F32),
        compiler_params=_params("parallel", "arbitrary"),
        name=name,
    )(a, w, res)


def _swiglu_up_kernel(a_ref, wg_ref, wu_ref, o_ref):
    a = a_ref[...]
    g = jnp.dot(a, wg_ref[...], preferred_element_type=F32)
    u = jnp.dot(a, wu_ref[...], preferred_element_type=F32)
    o_ref[...] = (_silu(g) * u).astype(o_ref.dtype)


def swiglu_up(h, w_gate_up):
    tm, tn = SWIGLU_UP_TILE
    T, K = h.shape
    F = w_gate_up.shape[1] // 2
    assert T % tm == 0 and F % tn == 0
    u_blk = F // tn
    return pl.pallas_call(
        _swiglu_up_kernel,
        grid=(T // tm, F // tn),
        in_specs=[pl.BlockSpec((tm, K), lambda i, j: (i, 0)),
                  pl.BlockSpec((K, tn), lambda i, j: (0, j)),
                  pl.BlockSpec((K, tn), lambda i, j: (0, u_blk + j))],
        out_specs=pl.BlockSpec((tm, tn), lambda i, j: (i, j)),
        out_shape=jax.ShapeDtypeStruct((T, F), BF16),
        compiler_params=_params("parallel", "arbitrary"),
        name="swiglu_up",
    )(h, w_gate_up, w_gate_up)


def _split2_bf16(x):
    hi = x.astype(BF16)
    return hi, (x - hi.astype(F32)).astype(BF16)


def _forget_gate(z, lb):
    half = 0.5 * (1.0 - lb)
    kk = half - half * jnp.tanh(0.5 * z)
    f = 1.0 - kk
    return f, jnp.log2(f), kk


def _hgrn_scan_tile(q_ref, z_ref, v_ref, lb_ref, st_ref, st_save, o_scr, vt_scr, emit, reverse):
    C = HG_CHUNK
    n_chunks = HG_TILE // C
    chunks = [slice(j * C, (j + 1) * C) for j in range(n_chunks)]
    row = lax.broadcasted_iota(jnp.int32, (HG_TILE, HG_TILE), 0)
    col = lax.broadcasted_iota(jnp.int32, (HG_TILE, HG_TILE), 1)
    visible = (col >= row) if reverse else (col <= row)
    chunk_bits = C.bit_length() - 1
    assert C == 1 << chunk_bits
    same_chunk = lax.shift_right_logical(row, chunk_bits) == lax.shift_right_logical(col, chunk_bits)
    keep = visible & same_chunk
    tri = jnp.where(keep, 1.0, 0.0).astype(BF16)
    end = 0 if reverse else C - 1
    mid = C // 2 if reverse else C // 2 - 1
    heads = [slice(g * HEAD, (g + 1) * HEAD) for g in range(HG_GROUP)]
    nt_dims = (((1,), (1,)), ((), ()))

    def rel(c, off):
        return jnp.concatenate([c[sl] - c[sl.start + off:sl.start + off + 1] for sl in chunks],
                               axis=0)

    st_save[...] = st_ref[...]

    keys, cums = [], []
    for hs in heads:
        _, logf, kk = _forget_gate(z_ref[:, hs], lb_ref[:, hs])
        keys.append(kk)
        cs = jnp.dot(tri, jnp.concatenate(_split2_bf16(logf), axis=1),
                     preferred_element_type=F32)
        cums.append(cs[:, :HEAD] + cs[:, HEAD:])
    mids = [rel(c, mid) for c in cums]
    worst = functools.reduce(jnp.maximum, [jnp.max(jnp.abs(a)) for a in mids])
    scores = []
    for g, hs in enumerate(heads):
        q_in = (q_ref[:, hs] * jnp.exp2(mids[g])).astype(BF16)
        k_in = (keys[g] * jnp.exp2(-mids[g])).astype(BF16)
        scores.append(lax.dot_general(q_in, k_in, nt_dims, preferred_element_type=F32))
    q_state, decay, update = [], {}, {}
    for g, hs in enumerate(heads):
        c, v = cums[g], v_ref[:, hs]
        k_st = (keys[g] * jnp.exp2(-rel(c, end))).astype(BF16)
        q_state.append((q_ref[:, hs] * jnp.exp2(c)).astype(BF16))
        for j, sl in enumerate(chunks):
            update[g, j] = jnp.dot(v[sl].T.astype(BF16), k_st[sl], preferred_element_type=F32)
            decay[g, j] = jnp.exp2(c[sl.start + end:sl.start + end + 1])
    for g, hs in enumerate(heads):
        s = jnp.where(keep, scores[g], 0.0).astype(BF16)
        o_scr[:, hs] = jnp.dot(s, v_ref[:, hs].astype(BF16), preferred_element_type=F32)
    order = range(n_chunks - 1, -1, -1) if reverse else range(n_chunks)
    for j in order:
        sl = chunks[j]
        for g, hs in enumerate(heads):
            st = st_ref[g]
            o_scr[sl, hs] += lax.dot_general(q_state[g][sl], st.astype(BF16), nt_dims,
                                             preferred_element_type=F32)
            st_ref[g] = st * decay[g, j] + update[g, j]
    emit()

    @pl.when(jnp.logical_not(worst < HG_MAX_EXPONENT))
    def _():
        st_ref[...] = st_save[...]
        lane = lax.broadcasted_iota(jnp.int32, (1, HG_TILE), 1)
        rows = lax.broadcasted_iota(jnp.int32, (HG_TILE, 1), 0)
        for g, hs in enumerate(heads):
            z, lb = z_ref[:, hs], lb_ref[:, hs]
            q = q_ref[:, hs]
            vt_scr[...] = v_ref[:, hs].T

            def body(i, carry):
                t = (HG_TILE - 1 - i) if reverse else i
                onehot = (lane == t).astype(F32)
                pick = lambda x: jnp.sum(jnp.where(rows == t, x, 0.0), axis=0, keepdims=True)
                f_t, _, k_t = _forget_gate(pick(z), lb)
                q_t = pick(q)
                v_col = jnp.sum(vt_scr[...] * onehot, axis=1, keepdims=True)
                st = st_ref[g] * f_t + v_col * k_t
                st_ref[g] = st
                o_col = jnp.sum(st * q_t, axis=1, keepdims=True)
                return carry + o_col * onehot

            ot = lax.fori_loop(0, HG_TILE, body, jnp.zeros((HEAD, HG_TILE), F32))
            o_scr[:, hs] = ot.T
        emit()


def _hgrn_fwd_kernel(q_ref, z_ref, v_ref, lb_ref, o_ref, st_ref, st_save, o_scr, vt_scr):
    @pl.when(pl.program_id(1) == 0)
    def _():
        st_ref[...] = jnp.zeros_like(st_ref)

    def emit():
        o_ref[...] = o_scr[...]

    _hgrn_scan_tile(q_ref, z_ref, v_ref, lb_ref, st_ref, st_save, o_scr, vt_scr, emit,
                    reverse=False)


def _hgrn_bwd_kernel(q_ref, z_ref, v_ref, lb_ref, of_ref, gate_ref, nw_ref, o_ref,
                     st_ref, st_save, o_scr, vt_scr):
    @pl.when(pl.program_id(1) == 0)
    def _():
        st_ref[...] = jnp.zeros_like(st_ref)

    def emit():
        for g in range(HG_GROUP):
            hs = slice(g * HEAD, (g + 1) * HEAD)
            o = of_ref[:, hs] + o_scr[:, hs]
            o = o * lax.rsqrt(jnp.mean(o * o, axis=-1, keepdims=True) + EPS) * nw_ref[:, hs]
            o_ref[:, hs] = (o * _silu(gate_ref[:, hs])).astype(o_ref.dtype)

    _hgrn_scan_tile(q_ref, z_ref, v_ref, lb_ref, st_ref, st_save, o_scr, vt_scr, emit,
                    reverse=True)


def hgrn2_bidirectional(proj, lb_f, lb_b, norm_w, cols):
    T = proj.shape[0]
    assert T % HG_TILE == 0 and HG_HEADS % HG_GROUP == 0
    n_tiles = T // HG_TILE
    W = HG_HEADS * HEAD
    GW = HG_GROUP * HEAD
    n_groups = HG_HEADS // HG_GROUP
    cq, czf, czb, ci, cg = (c // GW for c in cols)
    scratch = [pltpu.VMEM((HG_GROUP, HEAD, HEAD), F32),
               pltpu.VMEM((HG_GROUP, HEAD, HEAD), F32),
               pltpu.VMEM((HG_TILE, GW), F32),
               pltpu.VMEM((HEAD, HG_TILE), F32)]
    tile = lambda blk: pl.BlockSpec((HG_TILE, GW), lambda h, i: (i, blk + h))
    vec = pl.BlockSpec((1, GW), lambda h, i: (0, h))
    o_fwd = pl.pallas_call(
        _hgrn_fwd_kernel,
        grid=(n_groups, n_tiles),
        in_specs=[tile(cq), tile(czf), tile(ci), vec],
        out_specs=pl.BlockSpec((HG_TILE, GW), lambda h, i: (i, h)),
        out_shape=jax.ShapeDtypeStruct((T, W), F32),
        scratch_shapes=scratch,
        compiler_params=_params("parallel", "arbitrary"),
        name="hgrn_fwd",
    )(proj, proj, proj, lb_f.reshape(1, W))

    last = n_tiles - 1
    rtile = lambda blk: pl.BlockSpec((HG_TILE, GW), lambda h, i: (last - i, blk + h))
    return pl.pallas_call(
        _hgrn_bwd_kernel,
        grid=(n_groups, n_tiles),
        in_specs=[rtile(cq), rtile(czb), rtile(ci), vec, rtile(0), rtile(cg), vec],
        out_specs=pl.BlockSpec((HG_TILE, GW), lambda h, i: (last - i, h)),
        out_shape=jax.ShapeDtypeStruct((T, W), BF16),
        scratch_shapes=scratch,
        compiler_params=_params("parallel", "arbitrary"),
        name="hgrn_bwd",
    )(proj, proj, proj, lb_b.reshape(1, W), o_fwd, proj, norm_w.reshape(1, W))


def _rope_tables(T):
    rows = T // GRID_W
    row = jnp.repeat(jnp.arange(rows, dtype=F32), GRID_W)
    col = jnp.tile(jnp.arange(GRID_W, dtype=F32), rows)
    axis_dim = HEAD // 2
    inv = ROPE_THETA ** (-jnp.arange(0, axis_dim, 2, dtype=F32) / axis_dim)
    ang = jnp.concatenate([row[:, None] * inv, col[:, None] * inv], axis=-1)
    cos = jnp.repeat(jnp.cos(ang), 2, axis=-1)
    sin = jnp.repeat(jnp.sin(ang), 2, axis=-1)
    sign = jnp.tile(jnp.array([-1.0, 1.0], F32), HEAD // 2)
    return cos, sin * sign


def _norm_rope_head(x, w, cos, sin_signed, scale, axis):
    x = x * lax.rsqrt(jnp.mean(x * x, axis=axis, keepdims=True) + EPS) * w
    pos = lax.broadcasted_iota(jnp.int32, x.shape, axis)
    partner = jnp.where(pos % 2 == 0,
                        pltpu.roll(x, HEAD - 1, axis=axis),
                        pltpu.roll(x, 1, axis=axis))
    out = x * cos + partner * sin_signed
    return out * scale if scale is not None else out


def _qk_rope_kernel(q_ref, k_ref, v_ref, qw_ref, kw_ref, cos_ref, sin_ref, cos_t_ref, sin_t_ref,
                    qt_ref, ko_ref, vt_ref):
    cos_t, sin_t = cos_t_ref[...], sin_t_ref[...]
    for h in range(ATT_Q_HEADS):
        sl = slice(h * HEAD, (h + 1) * HEAD)
        q = _norm_rope_head(q_ref[:, sl].T, qw_ref[...], cos_t, sin_t, ATT_Q_SCALE, axis=0)
        qt_ref[sl, :] = q.astype(qt_ref.dtype)
    cos, sin = cos_ref[...], sin_ref[...]
    for h in range(ATT_KV_HEADS):
        sl = slice(h * HEAD, (h + 1) * HEAD)
        ko_ref[:, sl] = _norm_rope_head(k_ref[:, sl], kw_ref[...], cos, sin, None,
                                        axis=1).astype(ko_ref.dtype)
        vt_ref[0, sl, :] = v_ref[:, sl].T.astype(vt_ref.dtype)


def qk_norm_rope(proj, q_w, k_w, cols):
    T = proj.shape[0]
    tm = ATT_SUB_K
    assert T % tm == 0 and T % GRID_W == 0
    QW, KW = ATT_Q_HEADS * HEAD, ATT_KV_HEADS * HEAD
    cq, ck, cv = cols
    cos, sin = _rope_tables(T)
    q_w_col = jnp.broadcast_to(q_w.reshape(HEAD, 1), (HEAD, tm))
    return pl.pallas_call(
        _qk_rope_kernel,
        grid=(T // tm,),
        in_specs=[pl.BlockSpec((tm, QW), lambda i: (i, cq // QW)),
                  pl.BlockSpec((tm, KW), lambda i: (i, ck // KW)),
                  pl.BlockSpec((tm, KW), lambda i: (i, cv // KW)),
                  pl.BlockSpec((HEAD, tm), lambda i: (0, 0)),
                  pl.BlockSpec((1, HEAD), lambda i: (0, 0)),
                  pl.BlockSpec((tm, HEAD), lambda i: (i, 0)),
                  pl.BlockSpec((tm, HEAD), lambda i: (i, 0)),
                  pl.BlockSpec((HEAD, tm), lambda i: (0, i)),
                  pl.BlockSpec((HEAD, tm), lambda i: (0, i))],
        out_specs=[pl.BlockSpec((QW, tm), lambda i: (0, i)),
                   pl.BlockSpec((tm, KW), lambda i: (i, 0)),
                   pl.BlockSpec((1, KW, tm), lambda i: (i, 0, 0))],
        out_shape=[jax.ShapeDtypeStruct((QW, T), BF16),
                   jax.ShapeDtypeStruct((T, KW), BF16),
                   jax.ShapeDtypeStruct((T // tm, KW, tm), BF16)],
        compiler_params=_params("parallel"),
        name="qk_norm_rope",
    )(proj, proj, proj, q_w_col, k_w.reshape(1, HEAD), cos, sin, cos.T, sin.T)


def _row_norm_bound(norm_w, scale):
    margin = 1.0 + 2.0 ** -7
    return jnp.sqrt(jnp.float32(HEAD)) * jnp.max(jnp.abs(norm_w)) * (scale * margin)


def _flash_kernel(bounded_ref, qt_ref, k_ref, vt_ref, kmax_ref, *refs, n_side):
    side_in, o_ref, side_out = refs[:n_side], refs[n_side], refs[n_side + 1:2 * n_side + 1]
    m_scr, l_scr, acc_scr = refs[2 * n_side + 1:]
    j = pl.program_id(2)
    group = ATT_Q_HEADS // ATT_KV_HEADS

    def cast_side():
        for src, dst in zip(side_in, side_out):
            dst[...] = src[...].astype(dst.dtype)

    n_sub = k_ref.shape[0] // ATT_SUB_K
    bounded = bounded_ref[0] != 0

    def key_block(jj):
        k = k_ref[pl.ds(pl.multiple_of(jj * ATT_SUB_K, ATT_SUB_K), ATT_SUB_K), :]
        return k, vt_ref[jj]

    def logits(k, r):
        return jnp.dot(k, qt_ref[r * HEAD:(r + 1) * HEAD, :], preferred_element_type=F32)

    @pl.when(j == 0)
    def _():
        l_scr[...] = jnp.zeros_like(l_scr)
        acc_scr[...] = jnp.zeros_like(acc_scr)

        @pl.when(bounded)
        def _():
            for r in range(group):
                q = qt_ref[r * HEAD:(r + 1) * HEAD, :].astype(F32)
                m_scr[r] = jnp.sqrt(jnp.sum(q * q, axis=0, keepdims=True)) * kmax_ref[...]

        @pl.when(jnp.logical_not(bounded))
        def _():
            m_scr[...] = jnp.full_like(m_scr, -jnp.inf)

    @pl.when(bounded)
    def _():
        stages = [(jj, r) for jj in range(n_sub) for r in range(group)]
        cast_side()

        def stage_logits(stage):
            jj, r = stage
            return logits(k_ref[jj * ATT_SUB_K:(jj + 1) * ATT_SUB_K, :], r)

        st = stage_logits(stages[0])
        for idx, (jj, r) in enumerate(stages):
            st_next = stage_logits(stages[idx + 1]) if idx + 1 < len(stages) else None
            pt = jnp.exp2(st - m_scr[r])
            l_scr[r] += jnp.sum(pt, axis=0, keepdims=True)
            acc_scr[r] += jnp.dot(vt_ref[jj], pt.astype(BF16), preferred_element_type=F32)
            st = st_next

    @pl.when(jnp.logical_not(bounded))
    def _():
        cast_side()

        def sub_block(jj, carry):
            k, vt = key_block(jj)
            for r in range(group):
                st = logits(k, r)
                m_prev = m_scr[r]
                m_new = jnp.maximum(m_prev, jnp.max(st, axis=0, keepdims=True))
                alpha = jnp.exp2(m_prev - m_new)
                pt = jnp.exp2(st - m_new)
                l_scr[r] = alpha * l_scr[r] + jnp.sum(pt, axis=0, keepdims=True)
                acc_scr[r] = alpha * acc_scr[r] + jnp.dot(vt, pt.astype(BF16),
                                                          preferred_element_type=F32)
                m_scr[r] = m_new
            return carry

        lax.fori_loop(0, n_sub, sub_block, 0)

    @pl.when(j == pl.num_programs(2) - 1)
    def _():
        for r in range(group):
            o = acc_scr[r] / l_scr[r]
            o_ref[:, r * HEAD:(r + 1) * HEAD] = o.T.astype(o_ref.dtype)


def _slab_grid(rows, cols, n_slabs):
    for n_col in (1, 2, 4, 8, 16, 32):
        n_row = n_slabs // n_col
        if (n_row * n_col == n_slabs and rows % (n_row * BF16_SUBLANES) == 0
                and cols % (n_col * LANES) == 0):
            return n_row, n_col
    raise ValueError((rows, cols, n_slabs))


def flash_attention(qt, k, vt, q_max, k_max, cast=(), layer=0, tq=ATT_BLOCK_Q, tk=ATT_BLOCK_K):
    T = k.shape[0]
    tk = min(tk, T)
    assert T % tq == 0 and T % tk == 0 and tk % ATT_SUB_K == 0
    group = ATT_Q_HEADS // ATT_KV_HEADS
    GW = group * HEAD
    n_q = T // tq
    bounded = (q_max * k_max <= ATT_MAX_BOUNDED_LOGIT).astype(jnp.int32).reshape(1)
    kmax_row = jnp.full((1, tq), k_max, F32)

    side_in, side_out = [], []
    for w in cast:
        _, rows, cols = w.shape
        n_row, n_col = _slab_grid(rows, cols, ATT_KV_HEADS * n_q)
        block = (rows // n_row, cols // n_col)
        where = lambda g, i, j, b, n_col=n_col: ((g * n_q + i) // n_col, (g * n_q + i) % n_col)
        side_in.append(pl.BlockSpec((None,) + block,
                                    lambda g, i, j, b, where=where: (layer,) + where(g, i, j, b)))
        side_out.append(pl.BlockSpec(block, where))
    grid_spec = pltpu.PrefetchScalarGridSpec(
        num_scalar_prefetch=1,
        grid=(ATT_KV_HEADS, n_q, T // tk),
        in_specs=[pl.BlockSpec((GW, tq), lambda g, i, j, b: (g, i)),
                  pl.BlockSpec((tk, HEAD), lambda g, i, j, b: (j, g)),
                  pl.BlockSpec((tk // ATT_SUB_K, HEAD, ATT_SUB_K), lambda g, i, j, b: (j, g, 0)),
                  pl.BlockSpec((1, tq), lambda g, i, j, b: (0, 0))] + side_in,
        out_specs=[pl.BlockSpec((tq, GW), lambda g, i, j, b: (i, g))] + side_out,
        scratch_shapes=[pltpu.VMEM((group, 1, tq), F32),
                        pltpu.VMEM((group, 1, tq), F32),
                        pltpu.VMEM((group, HEAD, tq), F32)])
    return pl.pallas_call(
        functools.partial(_flash_kernel, n_side=len(cast)),
        grid_spec=grid_spec,
        out_shape=[jax.ShapeDtypeStruct((T, ATT_Q_HEADS * HEAD), BF16)]
                  + [jax.ShapeDtypeStruct(w.shape[1:], BF16) for w in cast],
        compiler_params=_params("parallel", "parallel", "arbitrary"),
        name="flash_attention",
    )(bounded, qt, k, vt, kmax_row, *cast)


def _encoder_layer(x, norm_mix_w, w_in, lb_f, lb_b, hg_norm_w, q_norm_w, k_norm_w,
                   norm_ffn_w, late_f32, layer, late_bf16):
    hg_w = HG_HEADS * HEAD
    q_w, kv_w = ATT_Q_HEADS * HEAD, ATT_KV_HEADS * HEAD
    D = x.shape[1]
    sizes = (hg_w, hg_w, hg_w, hg_w, hg_w, q_w, kv_w, kv_w, D, D)
    offs = np.concatenate([[0], np.cumsum(sizes)]).tolist()
    assert offs[-1] == w_in.shape[1]

    h = rmsnorm(x, norm_mix_w, BF16)
    proj = in_projection(h, w_in)
    o_hg = hgrn2_bidirectional(proj, lb_f, lb_b, hg_norm_w, offs[0:5])
    qt, k, vt = qk_norm_rope(proj, q_norm_w, k_norm_w, offs[5:8])
    o_att, *converted = flash_attention(qt, k, vt, _row_norm_bound(q_norm_w, ATT_Q_SCALE),
                                        _row_norm_bound(k_norm_w, 1.0),
                                        cast=() if late_bf16 else late_f32, layer=layer)
    late_bf16 = late_bf16 or tuple(converted)
    w_branch_a, w_branch_b, w_out, w_gate_up, w_down = late_bf16
    merged = gated_merge(o_hg, o_att, w_branch_a, w_branch_b, proj, offs[8], offs[9])
    x = residual_matmul(merged, w_out, x, OUT_PROJ_TILE, "out_projection")
    h = rmsnorm(x, norm_ffn_w, BF16)
    act = swiglu_up(h, w_gate_up)
    return residual_matmul(act, w_down, x, SWIGLU_DOWN_TILE, "swiglu_down"), late_bf16


def kernel(x_prompt, x_sample, norm_mix_w, w_in, lb_fwd, lb_bwd, hg_norm_w, q_norm_w, k_norm_w,
           w_branch_a, w_branch_b, w_out, norm_ffn_w, w_gate_up, w_down, norm_final_w):
    depth = w_in.shape[0]
    lbs_f = jnp.cumsum(jax.nn.softmax(lb_fwd.astype(F32), axis=0), axis=0)
    lbs_b = jnp.cumsum(jax.nn.softmax(lb_bwd.astype(F32), axis=0), axis=0)
    w_in = w_in.astype(BF16)
    late_f32 = (w_branch_a, w_branch_b, w_out, w_gate_up, w_down)
    late = [None] * depth

    def trunk(x):
        x = x.reshape(x.shape[-2], x.shape[-1])
        for l in range(depth):
            x, late[l] = _encoder_layer(x, norm_mix_w[l], w_in[l], lbs_f[l], lbs_b[l],
                                        hg_norm_w[l], q_norm_w[l], k_norm_w[l], norm_ffn_w[l],
                                        late_f32, l, late[l])
        return rmsnorm(x, norm_final_w, F32)

    assert x_prompt.shape[0] == 1 and x_sample.shape[0] == 1
    return trunk(x_prompt)[None], trunk(x_sample)[None]
```

```python
import functools

import jax
import jax.numpy as jnp
import numpy as np
from jax import lax
from jax.experimental import pallas as pl
from jax.experimental.pallas import tpu as pltpu

F32 = jnp.float32
BF16 = jnp.bfloat16

EPS = 1e-6
LANES = 128
BF16_SUBLANES = 16
HEAD = 128
HG_HEADS = 16
ATT_Q_HEADS = 16
ATT_KV_HEADS = 4
GRID_W = 64
ROPE_THETA = 10000.0

V7X_VMEM_LIMIT_BYTES = 56 * 1024 * 1024

IN_PROJ_TILE = (1024, 1024)
MERGE_TILE = (1024, 512)
OUT_PROJ_TILE = (1024, 1024)
SWIGLU_UP_TILE = (2048, 256)
SWIGLU_DOWN_TILE = (512, 512)
NORM_ROWS = 512

HG_CHUNK = 64
HG_TILE = 256
HG_GROUP = 16
HG_MAX_EXPONENT = 115.0

ATT_SUB_K = 512
ATT_BLOCK_Q = 512
ATT_BLOCK_K = 8192
ATT_Q_SCALE = float(np.log2(np.e) / np.sqrt(np.float32(HEAD)))
ATT_MAX_BOUNDED_LOGIT = 60.0


def _params(*semantics):
    return pltpu.CompilerParams(dimension_semantics=semantics,
                                vmem_limit_bytes=V7X_VMEM_LIMIT_BYTES)


def _sigmoid(x):
    return 0.5 + 0.5 * jnp.tanh(0.5 * x)


def _silu(x):
    return x * _sigmoid(x)


def _rmsnorm_kernel(x_ref, w_ref, o_ref):
    x = x_ref[...]
    inv = lax.rsqrt(jnp.mean(x * x, axis=-1, keepdims=True) + EPS)
    o_ref[...] = (x * inv * w_ref[...]).astype(o_ref.dtype)


def rmsnorm(x, w, out_dtype):
    T, D = x.shape
    tm = NORM_ROWS
    assert T % tm == 0
    return pl.pallas_call(
        _rmsnorm_kernel,
        grid=(T // tm,),
        in_specs=[pl.BlockSpec((tm, D), lambda i: (i, 0)),
                  pl.BlockSpec((1, D), lambda i: (0, 0))],
        out_specs=pl.BlockSpec((tm, D), lambda i: (i, 0)),
        out_shape=jax.ShapeDtypeStruct((T, D), out_dtype),
        compiler_params=_params("parallel"),
        name="rmsnorm",
    )(x, w.reshape(1, D))


def _proj_kernel(a_ref, w_ref, o_ref):
    o_ref[...] = jnp.dot(a_ref[...], w_ref[...], preferred_element_type=F32)


def in_projection(h, w):
    tm, tn = IN_PROJ_TILE
    T, K = h.shape
    N = w.shape[1]
    assert T % tm == 0 and N % tn == 0
    return pl.pallas_call(
        _proj_kernel,
        grid=(T // tm, N // tn),
        in_specs=[pl.BlockSpec((tm, K), lambda i, j: (i, 0)),
                  pl.BlockSpec((K, tn), lambda i, j: (0, j))],
        out_specs=pl.BlockSpec((tm, tn), lambda i, j: (i, j)),
        out_shape=jax.ShapeDtypeStruct((T, N), F32),
        compiler_params=_params("parallel", "arbitrary"),
        name="in_projection",
    )(h, w)


def _merge_kernel(oa_ref, ob_ref, wa_ref, wb_ref, ga_ref, gb_ref, o_ref):
    ya = jnp.dot(oa_ref[...], wa_ref[...], preferred_element_type=F32)
    yb = jnp.dot(ob_ref[...], wb_ref[...], preferred_element_type=F32)
    o_ref[...] = (_sigmoid(ga_ref[...]) * ya + _sigmoid(gb_ref[...]) * yb).astype(o_ref.dtype)


def gated_merge(o_hg, o_att, w_a, w_b, proj, ga_col, gb_col):
    tm, tn = MERGE_TILE
    T, K = o_hg.shape
    D = w_a.shape[1]
    assert T % tm == 0 and D % tn == 0 and ga_col % tn == 0 and gb_col % tn == 0
    ga_blk, gb_blk = ga_col // tn, gb_col // tn
    return pl.pallas_call(
        _merge_kernel,
        grid=(T // tm, D // tn),
        in_specs=[pl.BlockSpec((tm, K), lambda i, j: (i, 0)),
                  pl.BlockSpec((tm, K), lambda i, j: (i, 0)),
                  pl.BlockSpec((K, tn), lambda i, j: (0, j)),
                  pl.BlockSpec((K, tn), lambda i, j: (0, j)),
                  pl.BlockSpec((tm, tn), lambda i, j: (i, ga_blk + j)),
                  pl.BlockSpec((tm, tn), lambda i, j: (i, gb_blk + j))],
        out_specs=pl.BlockSpec((tm, tn), lambda i, j: (i, j)),
        out_shape=jax.ShapeDtypeStruct((T, D), BF16),
        compiler_params=_params("parallel", "arbitrary"),
        name="gated_merge",
    )(o_hg, o_att, w_a, w_b, proj, proj)


def _residual_mm_kernel(a_ref, w_ref, r_ref, *refs):
    if len(refs) == 1:
        (o_ref,) = refs
    else:
        nx_ref, nw_ref, o_ref, nh_ref = refs
        _rmsnorm_kernel(nx_ref, nw_ref, nh_ref)
    o_ref[...] = r_ref[...] + jnp.dot(a_ref[...], w_ref[...], preferred_element_type=F32)


def residual_matmul(a, w, res, tile, name, norm_next=None):
    tm, tn = tile
    T, K = a.shape
    N = w.shape[1]
    assert T % tm == 0 and N % tn == 0
    n_j = N // tn
    in_specs = [pl.BlockSpec((tm, K), lambda i, j: (i, 0)),
                pl.BlockSpec((K, tn), lambda i, j: (0, j)),
                pl.BlockSpec((tm, tn), lambda i, j: (i, j))]
    out_specs = [pl.BlockSpec((tm, tn), lambda i, j: (i, j))]
    out_shape = [jax.ShapeDtypeStruct((T, N), F32)]
    operands = [a, w, res]
    if norm_next is not None:
        x2, gain = norm_next
        T2, D2 = x2.shape
        rows = T2 // ((T // tm) * n_j)
        assert rows * (T // tm) * n_j == T2 and rows % BF16_SUBLANES == 0
        slab = pl.BlockSpec((rows, D2), lambda i, j: (i * n_j + j, 0))
        in_specs += [slab, pl.BlockSpec((1, D2), lambda i, j: (0, 0))]
        out_specs.append(slab)
        out_shape.append(jax.ShapeDtypeStruct((T2, D2), BF16))
        operands += [x2, gain.reshape(1, D2)]
    outs = pl.pallas_call(
        _residual_mm_kernel,
        grid=(T // tm, n_j),
        in_specs=in_specs,
        out_specs=out_specs,
        out_shape=out_shape,
        compiler_params=_params("parallel", "arbitrary"),
        name=name,
    )(*operands)
    return outs[0] if norm_next is None else tuple(outs)


def _swiglu_up_kernel(a_ref, wg_ref, wu_ref, o_ref):
    a = a_ref[...]
    g = jnp.dot(a, wg_ref[...], preferred_element_type=F32)
    u = jnp.dot(a, wu_ref[...], preferred_element_type=F32)
    o_ref[...] = (_silu(g) * u).astype(o_ref.dtype)


def swiglu_up(h, w_gate_up):
    tm, tn = SWIGLU_UP_TILE
    T, K = h.shape
    F = w_gate_up.shape[1] // 2
    assert T % tm == 0 and F % tn == 0
    u_blk = F // tn
    return pl.pallas_call(
        _swiglu_up_kernel,
        grid=(T // tm, F // tn),
        in_specs=[pl.BlockSpec((tm, K), lambda i, j: (i, 0)),
                  pl.BlockSpec((K, tn), lambda i, j: (0, j)),
                  pl.BlockSpec((K, tn), lambda i, j: (0, u_blk + j))],
        out_specs=pl.BlockSpec((tm, tn), lambda i, j: (i, j)),
        out_shape=jax.ShapeDtypeStruct((T, F), BF16),
        compiler_params=_params("parallel", "arbitrary"),
        name="swiglu_up",
    )(h, w_gate_up, w_gate_up)


def _split2_bf16(x):
    hi = x.astype(BF16)
    return hi, (x - hi.astype(F32)).astype(BF16)


def _forget_gate(z, lb):
    half = 0.5 * (1.0 - lb)
    kk = half - half * jnp.tanh(0.5 * z)
    f = 1.0 - kk
    return f, jnp.log2(f), kk


def _hgrn_scan_tile(q_ref, z_ref, v_ref, lb_ref, st_ref, st_save, o_scr, vt_scr, emit, reverse):
    C = HG_CHUNK
    n_chunks = HG_TILE // C
    chunks = [slice(j * C, (j + 1) * C) for j in range(n_chunks)]
    row = lax.broadcasted_iota(jnp.int32, (HG_TILE, HG_TILE), 0)
    col = lax.broadcasted_iota(jnp.int32, (HG_TILE, HG_TILE), 1)
    visible = (col >= row) if reverse else (col <= row)
    chunk_bits = C.bit_length() - 1
    assert C == 1 << chunk_bits
    same_chunk = lax.shift_right_logical(row, chunk_bits) == lax.shift_right_logical(col, chunk_bits)
    keep = visible & same_chunk
    tri = jnp.where(keep, 1.0, 0.0).astype(BF16)
    end = 0 if reverse else C - 1
    mid = C // 2 if reverse else C // 2 - 1
    heads = [slice(g * HEAD, (g + 1) * HEAD) for g in range(HG_GROUP)]
    nt_dims = (((1,), (1,)), ((), ()))

    def rel(c, off):
        return jnp.concatenate([c[sl] - c[sl.start + off:sl.start + off + 1] for sl in chunks],
                               axis=0)

    st_save[...] = st_ref[...]

    keys, cums = [], []
    for hs in heads:
        _, logf, kk = _forget_gate(z_ref[:, hs], lb_ref[:, hs])
        keys.append(kk)
        cs = jnp.dot(tri, jnp.concatenate(_split2_bf16(logf), axis=1),
                     preferred_element_type=F32)
        cums.append(cs[:, :HEAD] + cs[:, HEAD:])
    mids = [rel(c, mid) for c in cums]
    worst = functools.reduce(jnp.maximum, [jnp.max(jnp.abs(a)) for a in mids])
    scores = []
    for g, hs in enumerate(heads):
        q_in = (q_ref[:, hs] * jnp.exp2(mids[g])).astype(BF16)
        k_in = (keys[g] * jnp.exp2(-mids[g])).astype(BF16)
        scores.append(lax.dot_general(q_in, k_in, nt_dims, preferred_element_type=F32))
    q_state, decay, update = [], {}, {}
    for g, hs in enumerate(heads):
        c, v = cums[g], v_ref[:, hs]
        k_st = (keys[g] * jnp.exp2(-rel(c, end))).astype(BF16)
        q_state.append((q_ref[:, hs] * jnp.exp2(c)).astype(BF16))
        for j, sl in enumerate(chunks):
            update[g, j] = jnp.dot(v[sl].T.astype(BF16), k_st[sl], preferred_element_type=F32)
            decay[g, j] = jnp.exp2(c[sl.start + end:sl.start + end + 1])
    for g, hs in enumerate(heads):
        s = jnp.where(keep, scores[g], 0.0).astype(BF16)
        o_scr[:, hs] = jnp.dot(s, v_ref[:, hs].astype(BF16), preferred_element_type=F32)
    order = range(n_chunks - 1, -1, -1) if reverse else range(n_chunks)
    for j in order:
        sl = chunks[j]
        for g, hs in enumerate(heads):
            st = st_ref[g]
            o_scr[sl, hs] += lax.dot_general(q_state[g][sl], st.astype(BF16), nt_dims,
                                             preferred_element_type=F32)
            st_ref[g] = st * decay[g, j] + update[g, j]
    emit()

    @pl.when(jnp.logical_not(worst < HG_MAX_EXPONENT))
    def _():
        st_ref[...] = st_save[...]
        lane = lax.broadcasted_iota(jnp.int32, (1, HG_TILE), 1)
        rows = lax.broadcasted_iota(jnp.int32, (HG_TILE, 1), 0)
        for g, hs in enumerate(heads):
            z, lb = z_ref[:, hs], lb_ref[:, hs]
            q = q_ref[:, hs]
            vt_scr[...] = v_ref[:, hs].T

            def body(i, carry):
                t = (HG_TILE - 1 - i) if reverse else i
                onehot = (lane == t).astype(F32)
                pick = lambda x: jnp.sum(jnp.where(rows == t, x, 0.0), axis=0, keepdims=True)
                f_t, _, k_t = _forget_gate(pick(z), lb)
                q_t = pick(q)
                v_col = jnp.sum(vt_scr[...] * onehot, axis=1, keepdims=True)
                st = st_ref[g] * f_t + v_col * k_t
                st_ref[g] = st
                o_col = jnp.sum(st * q_t, axis=1, keepdims=True)
                return carry + o_col * onehot

            ot = lax.fori_loop(0, HG_TILE, body, jnp.zeros((HEAD, HG_TILE), F32))
            o_scr[:, hs] = ot.T
        emit()


def _hgrn_fwd_kernel(q_ref, z_ref, v_ref, lb_ref, o_ref, st_ref, st_save, o_scr, vt_scr):
    @pl.when(pl.program_id(1) == 0)
    def _():
        st_ref[...] = jnp.zeros_like(st_ref)

    def emit():
        o_ref[...] = o_scr[...]

    _hgrn_scan_tile(q_ref, z_ref, v_ref, lb_ref, st_ref, st_save, o_scr, vt_scr, emit,
                    reverse=False)


def _hgrn_bwd_kernel(q_ref, z_ref, v_ref, lb_ref, of_ref, gate_ref, nw_ref, o_ref,
                     st_ref, st_save, o_scr, vt_scr):
    @pl.when(pl.program_id(1) == 0)
    def _():
        st_ref[...] = jnp.zeros_like(st_ref)

    def emit():
        for g in range(HG_GROUP):
            hs = slice(g * HEAD, (g + 1) * HEAD)
            o = of_ref[:, hs] + o_scr[:, hs]
            o = o * lax.rsqrt(jnp.mean(o * o, axis=-1, keepdims=True) + EPS) * nw_ref[:, hs]
            o_ref[:, hs] = (o * _silu(gate_ref[:, hs])).astype(o_ref.dtype)

    _hgrn_scan_tile(q_ref, z_ref, v_ref, lb_ref, st_ref, st_save, o_scr, vt_scr, emit,
                    reverse=True)


def hgrn2_bidirectional(proj, lb_f, lb_b, norm_w, cols):
    T = proj.shape[0]
    assert T % HG_TILE == 0 and HG_HEADS % HG_GROUP == 0
    n_tiles = T // HG_TILE
    W = HG_HEADS * HEAD
    GW = HG_GROUP * HEAD
    n_groups = HG_HEADS // HG_GROUP
    cq, czf, czb, ci, cg = (c // GW for c in cols)
    scratch = [pltpu.VMEM((HG_GROUP, HEAD, HEAD), F32),
               pltpu.VMEM((HG_GROUP, HEAD, HEAD), F32),
               pltpu.VMEM((HG_TILE, GW), F32),
               pltpu.VMEM((HEAD, HG_TILE), F32)]
    tile = lambda blk: pl.BlockSpec((HG_TILE, GW), lambda h, i: (i, blk + h))
    vec = pl.BlockSpec((1, GW), lambda h, i: (0, h))
    o_fwd = pl.pallas_call(
        _hgrn_fwd_kernel,
        grid=(n_groups, n_tiles),
        in_specs=[tile(cq), tile(czf), tile(ci), vec],
        out_specs=pl.BlockSpec((HG_TILE, GW), lambda h, i: (i, h)),
        out_shape=jax.ShapeDtypeStruct((T, W), F32),
        scratch_shapes=scratch,
        compiler_params=_params("parallel", "arbitrary"),
        name="hgrn_fwd",
    )(proj, proj, proj, lb_f.reshape(1, W))

    last = n_tiles - 1
    rtile = lambda blk: pl.BlockSpec((HG_TILE, GW), lambda h, i: (last - i, blk + h))
    return pl.pallas_call(
        _hgrn_bwd_kernel,
        grid=(n_groups, n_tiles),
        in_specs=[rtile(cq), rtile(czb), rtile(ci), vec, rtile(0), rtile(cg), vec],
        out_specs=pl.BlockSpec((HG_TILE, GW), lambda h, i: (last - i, h)),
        out_shape=jax.ShapeDtypeStruct((T, W), BF16),
        scratch_shapes=scratch,
        compiler_params=_params("parallel", "arbitrary"),
        name="hgrn_bwd",
    )(proj, proj, proj, lb_b.reshape(1, W), o_fwd, proj, norm_w.reshape(1, W))


def _rope_tables(T):
    rows = T // GRID_W
    row = jnp.repeat(jnp.arange(rows, dtype=F32), GRID_W)
    col = jnp.tile(jnp.arange(GRID_W, dtype=F32), rows)
    axis_dim = HEAD // 2
    inv = ROPE_THETA ** (-jnp.arange(0, axis_dim, 2, dtype=F32) / axis_dim)
    ang = jnp.concatenate([row[:, None] * inv, col[:, None] * inv], axis=-1)
    cos = jnp.repeat(jnp.cos(ang), 2, axis=-1)
    sin = jnp.repeat(jnp.sin(ang), 2, axis=-1)
    sign = jnp.tile(jnp.array([-1.0, 1.0], F32), HEAD // 2)
    return cos, sin * sign


def _norm_rope_head(x, w, cos, sin_signed, scale, axis):
    x = x * lax.rsqrt(jnp.mean(x * x, axis=axis, keepdims=True) + EPS) * w
    pos = lax.broadcasted_iota(jnp.int32, x.shape, axis)
    partner = jnp.where(pos % 2 == 0,
                        pltpu.roll(x, HEAD - 1, axis=axis),
                        pltpu.roll(x, 1, axis=axis))
    out = x * cos + partner * sin_signed
    return out * scale if scale is not None else out


def _qk_rope_kernel(q_ref, k_ref, v_ref, qw_ref, kw_ref, cos_ref, sin_ref, cos_t_ref, sin_t_ref,
                    qt_ref, ko_ref, vt_ref):
    cos_t, sin_t = cos_t_ref[...], sin_t_ref[...]
    for h in range(ATT_Q_HEADS):
        sl = slice(h * HEAD, (h + 1) * HEAD)
        q = _norm_rope_head(q_ref[:, sl].T, qw_ref[...], cos_t, sin_t, ATT_Q_SCALE, axis=0)
        qt_ref[sl, :] = q.astype(qt_ref.dtype)
    cos, sin = cos_ref[...], sin_ref[...]
    for h in range(ATT_KV_HEADS):
        sl = slice(h * HEAD, (h + 1) * HEAD)
        ko_ref[:, sl] = _norm_rope_head(k_ref[:, sl], kw_ref[...], cos, sin, None,
                                        axis=1).astype(ko_ref.dtype)
        vt_ref[0, sl, :] = v_ref[:, sl].T.astype(vt_ref.dtype)


def qk_norm_rope(proj, q_w, k_w, cols):
    T = proj.shape[0]
    tm = ATT_SUB_K
    assert T % tm == 0 and T % GRID_W == 0
    QW, KW = ATT_Q_HEADS * HEAD, ATT_KV_HEADS * HEAD
    cq, ck, cv = cols
    cos, sin = _rope_tables(T)
    q_w_col = jnp.broadcast_to(q_w.reshape(HEAD, 1), (HEAD, tm))
    return pl.pallas_call(
        _qk_rope_kernel,
        grid=(T // tm,),
        in_specs=[pl.BlockSpec((tm, QW), lambda i: (i, cq // QW)),
                  pl.BlockSpec((tm, KW), lambda i: (i, ck // KW)),
                  pl.BlockSpec((tm, KW), lambda i: (i, cv // KW)),
                  pl.BlockSpec((HEAD, tm), lambda i: (0, 0)),
                  pl.BlockSpec((1, HEAD), lambda i: (0, 0)),
                  pl.BlockSpec((tm, HEAD), lambda i: (i, 0)),
                  pl.BlockSpec((tm, HEAD), lambda i: (i, 0)),
                  pl.BlockSpec((HEAD, tm), lambda i: (0, i)),
                  pl.BlockSpec((HEAD, tm), lambda i: (0, i))],
        out_specs=[pl.BlockSpec((QW, tm), lambda i: (0, i)),
                   pl.BlockSpec((tm, KW), lambda i: (i, 0)),
                   pl.BlockSpec((1, KW, tm), lambda i: (i, 0, 0))],
        out_shape=[jax.ShapeDtypeStruct((QW, T), BF16),
                   jax.ShapeDtypeStruct((T, KW), BF16),
                   jax.ShapeDtypeStruct((T // tm, KW, tm), BF16)],
        compiler_params=_params("parallel"),
        name="qk_norm_rope",
    )(proj, proj, proj, q_w_col, k_w.reshape(1, HEAD), cos, sin, cos.T, sin.T)


def _row_norm_bound(norm_w, scale):
    margin = 1.0 + 2.0 ** -7
    return jnp.sqrt(jnp.float32(HEAD)) * jnp.max(jnp.abs(norm_w)) * (scale * margin)


def _flash_kernel(bounded_ref, qt_ref, k_ref, vt_ref, kmax_ref, *refs, n_side):
    side_in, o_ref, side_out = refs[:n_side], refs[n_side], refs[n_side + 1:2 * n_side + 1]
    m_scr, l_scr, acc_scr = refs[2 * n_side + 1:]
    j = pl.program_id(2)
    group = ATT_Q_HEADS // ATT_KV_HEADS

    def cast_side():
        for src, dst in zip(side_in, side_out):
            dst[...] = src[...].astype(dst.dtype)

    n_sub = k_ref.shape[0] // ATT_SUB_K
    bounded = bounded_ref[0] != 0

    def key_block(jj):
        k = k_ref[pl.ds(pl.multiple_of(jj * ATT_SUB_K, ATT_SUB_K), ATT_SUB_K), :]
        return k, vt_ref[jj]

    def logits(k, r):
        return jnp.dot(k, qt_ref[r * HEAD:(r + 1) * HEAD, :], preferred_element_type=F32)

    @pl.when(j == 0)
    def _():
        l_scr[...] = jnp.zeros_like(l_scr)
        acc_scr[...] = jnp.zeros_like(acc_scr)

        @pl.when(bounded)
        def _():
            for r in range(group):
                q = qt_ref[r * HEAD:(r + 1) * HEAD, :].astype(F32)
                m_scr[r] = jnp.sqrt(jnp.sum(q * q, axis=0, keepdims=True)) * kmax_ref[...]

        @pl.when(jnp.logical_not(bounded))
        def _():
            m_scr[...] = jnp.full_like(m_scr, -jnp.inf)

    @pl.when(bounded)
    def _():
        stages = [(jj, r) for jj in range(n_sub) for r in range(group)]
        cast_side()

        def stage_logits(stage):
            jj, r = stage
            return logits(k_ref[jj * ATT_SUB_K:(jj + 1) * ATT_SUB_K, :], r)

        st = stage_logits(stages[0])
        for idx, (jj, r) in enumerate(stages):
            st_next = stage_logits(stages[idx + 1]) if idx + 1 < len(stages) else None
            pt = jnp.exp2(st - m_scr[r])
            l_scr[r] += jnp.sum(pt, axis=0, keepdims=True)
            acc_scr[r] += jnp.dot(vt_ref[jj], pt.astype(BF16), preferred_element_type=F32)
            st = st_next

    @pl.when(jnp.logical_not(bounded))
    def _():
        cast_side()

        def sub_block(jj, carry):
            k, vt = key_block(jj)
            for r in range(group):
                st = logits(k, r)
                m_prev = m_scr[r]
                m_new = jnp.maximum(m_prev, jnp.max(st, axis=0, keepdims=True))
                alpha = jnp.exp2(m_prev - m_new)
                pt = jnp.exp2(st - m_new)
                l_scr[r] = alpha * l_scr[r] + jnp.sum(pt, axis=0, keepdims=True)
                acc_scr[r] = alpha * acc_scr[r] + jnp.dot(vt, pt.astype(BF16),
                                                          preferred_element_type=F32)
                m_scr[r] = m_new
            return carry

        lax.fori_loop(0, n_sub, sub_block, 0)

    @pl.when(j == pl.num_programs(2) - 1)
    def _():
        for r in range(group):
            o = acc_scr[r] / l_scr[r]
            o_ref[:, r * HEAD:(r + 1) * HEAD] = o.T.astype(o_ref.dtype)


def _slab_grid(rows, cols, n_slabs):
    for n_col in (1, 2, 4, 8, 16, 32):
        n_row = n_slabs // n_col
        if (n_row * n_col == n_slabs and rows % (n_row * BF16_SUBLANES) == 0
                and cols % (n_col * LANES) == 0):
            return n_row, n_col
    raise ValueError((rows, cols, n_slabs))


def flash_attention(qt, k, vt, q_max, k_max, cast=(), layer=0, tq=ATT_BLOCK_Q, tk=ATT_BLOCK_K):
    T = k.shape[0]
    tk = min(tk, T)
    assert T % tq == 0 and T % tk == 0 and tk % ATT_SUB_K == 0
    group = ATT_Q_HEADS // ATT_KV_HEADS
    GW = group * HEAD
    n_q = T // tq
    bounded = (q_max * k_max <= ATT_MAX_BOUNDED_LOGIT).astype(jnp.int32).reshape(1)
    kmax_row = jnp.full((1, tq), k_max, F32)

    side_in, side_out = [], []
    for w in cast:
        _, rows, cols = w.shape
        n_row, n_col = _slab_grid(rows, cols, ATT_KV_HEADS * n_q)
        block = (rows // n_row, cols // n_col)
        where = lambda g, i, j, b, n_col=n_col: ((g * n_q + i) // n_col, (g * n_q + i) % n_col)
        side_in.append(pl.BlockSpec((None,) + block,
                                    lambda g, i, j, b, where=where: (layer,) + where(g, i, j, b)))
        side_out.append(pl.BlockSpec(block, where))
    grid_spec = pltpu.PrefetchScalarGridSpec(
        num_scalar_prefetch=1,
        grid=(ATT_KV_HEADS, n_q, T // tk),
        in_specs=[pl.BlockSpec((GW, tq), lambda g, i, j, b: (g, i)),
                  pl.BlockSpec((tk, HEAD), lambda g, i, j, b: (j, g)),
                  pl.BlockSpec((tk // ATT_SUB_K, HEAD, ATT_SUB_K), lambda g, i, j, b: (j, g, 0)),
                  pl.BlockSpec((1, tq), lambda g, i, j, b: (0, 0))] + side_in,
        out_specs=[pl.BlockSpec((tq, GW), lambda g, i, j, b: (i, g))] + side_out,
        scratch_shapes=[pltpu.VMEM((group, 1, tq), F32),
                        pltpu.VMEM((group, 1, tq), F32),
                        pltpu.VMEM((group, HEAD, tq), F32)])
    return pl.pallas_call(
        functools.partial(_flash_kernel, n_side=len(cast)),
        grid_spec=grid_spec,
        out_shape=[jax.ShapeDtypeStruct((T, ATT_Q_HEADS * HEAD), BF16)]
                  + [jax.ShapeDtypeStruct(w.shape[1:], BF16) for w in cast],
        compiler_params=_params("parallel", "parallel", "arbitrary"),
        name="flash_attention",
    )(bounded, qt, k, vt, kmax_row, *cast)


def _encoder_layer(x, norm_mix_w, w_in, lb_f, lb_b, hg_norm_w, q_norm_w, k_norm_w,
                   norm_ffn_w, late_f32, layer, late_bf16, h=None, norm_next=None):
    hg_w = HG_HEADS * HEAD
    q_w, kv_w = ATT_Q_HEADS * HEAD, ATT_KV_HEADS * HEAD
    D = x.shape[1]
    sizes = (hg_w, hg_w, hg_w, hg_w, hg_w, q_w, kv_w, kv_w, D, D)
    offs = np.concatenate([[0], np.cumsum(sizes)]).tolist()
    assert offs[-1] == w_in.shape[1]

    if h is None:
        h = rmsnorm(x, norm_mix_w, BF16)
    proj = in_projection(h, w_in)
    o_hg = hgrn2_bidirectional(proj, lb_f, lb_b, hg_norm_w, offs[0:5])
    qt, k, vt = qk_norm_rope(proj, q_norm_w, k_norm_w, offs[5:8])
    o_att, *converted = flash_attention(qt, k, vt, _row_norm_bound(q_norm_w, ATT_Q_SCALE),
                                        _row_norm_bound(k_norm_w, 1.0),
                                        cast=() if late_bf16 else late_f32, layer=layer)
    late_bf16 = late_bf16 or tuple(converted)
    w_branch_a, w_branch_b, w_out, w_gate_up, w_down = late_bf16
    merged = gated_merge(o_hg, o_att, w_branch_a, w_branch_b, proj, offs[8], offs[9])
    x = residual_matmul(merged, w_out, x, OUT_PROJ_TILE, "out_projection")
    h = rmsnorm(x, norm_ffn_w, BF16)
    act = swiglu_up(h, w_gate_up)
    out = residual_matmul(act, w_down, x, SWIGLU_DOWN_TILE, "swiglu_down", norm_next=norm_next)
    return (out, late_bf16, None) if norm_next is None else (out[0], late_bf16, out[1])


def kernel(x_prompt, x_sample, norm_mix_w, w_in, lb_fwd, lb_bwd, hg_norm_w, q_norm_w, k_norm_w,
           w_branch_a, w_branch_b, w_out, norm_ffn_w, w_gate_up, w_down, norm_final_w):
    depth = w_in.shape[0]
    lbs_f = jnp.cumsum(jax.nn.softmax(lb_fwd.astype(F32), axis=0), axis=0)
    lbs_b = jnp.cumsum(jax.nn.softmax(lb_bwd.astype(F32), axis=0), axis=0)
    w_in = w_in.astype(BF16)
    late_f32 = (w_branch_a, w_branch_b, w_out, w_gate_up, w_down)
    late = [None] * depth

    def trunk(x, h_first=None, next_x=None):
        h_next = None
        for l in range(depth):
            last = l == depth - 1
            x, late[l], h_next = _encoder_layer(
                x, norm_mix_w[l], w_in[l], lbs_f[l], lbs_b[l], hg_norm_w[l], q_norm_w[l],
                k_norm_w[l], norm_ffn_w[l], late_f32, l, late[l],
                h=h_first if l == 0 else None,
                norm_next=(next_x, norm_mix_w[0]) if last and next_x is not None else None)
        return rmsnorm(x, norm_final_w, F32), h_next

    assert x_prompt.shape[0] == 1 and x_sample.shape[0] == 1
    x_p, x_s = (x.reshape(x.shape[-2], x.shape[-1]) for x in (x_prompt, x_sample))
    y_p, h_s = trunk(x_p, next_x=x_s)
    y_s, _ = trunk(x_s, h_first=h_s)
    return y_p[None], y_s[None]
```
